```python
import jax
import jax.numpy as jnp
from jax import lax
import numpy as np

D_MODEL = 1024
BATCH = 8
SEQ = 8192
DEPTH = 2

N_META = 16
HG_HEADS = 4
HG_DK = 128
HG_DV = 128
HG_CHUNK = 64
AT_HEADS = 8
AT_DH = 64
AT_QRANK = 256
IDX_HEADS = 4
IDX_DIM = 64
TOPK_MAX = 256
Q_BLOCK = 128
CV_CH = 512
CV_WIDTH = 31
FF_DIM = 2816
FF_CONV = 3
N_BRANCH = 3
EPS = 1e-6
MASK_BIG = 1e30

IN_SPLITS = (
    HG_HEADS * HG_DK,
    HG_HEADS * HG_DK,
    HG_HEADS * HG_DV,
    HG_HEADS * HG_DV,
    AT_QRANK,
    AT_DH,
    AT_DH,
    IDX_DIM,
    IDX_HEADS,
    2 * CV_CH,
    N_BRANCH * D_MODEL,
)
N_IN = 2 * HG_HEADS * HG_DK + 2 * HG_HEADS * HG_DV + AT_QRANK + 2 * AT_DH + IDX_DIM + IDX_HEADS + 2 * CV_CH + N_BRANCH * D_MODEL

kernel_name = 'hybrid_hgrn2_dsa_conformer_block'


def rms_norm(x, g):
    x32 = x.astype(jnp.float32)
    y = x32 * lax.rsqrt(jnp.mean(x32 * x32, axis=-1, keepdims=True) + EPS)
    return (y * g.astype(jnp.float32)).astype(x.dtype)


def layer_norm(x, g, b):
    x32 = x.astype(jnp.float32)
    mu = jnp.mean(x32, axis=-1, keepdims=True)
    xc = x32 - mu
    y = xc * lax.rsqrt(jnp.mean(xc * xc, axis=-1, keepdims=True) + EPS)
    return (y * g.astype(jnp.float32) + b.astype(jnp.float32)).astype(x.dtype)


def causal_dwconv(x, w, b):
    K = w.shape[0]
    y = lax.conv_general_dilated(
        x, w[:, None, :].astype(x.dtype), window_strides=(1,), padding=((K - 1, 0),),
        dimension_numbers=('NWC', 'WIO', 'NWC'), feature_group_count=x.shape[-1])
    return y + b.astype(x.dtype)


def _split(z, sizes):
    cuts = np.cumsum(np.array(sizes))[:-1].tolist()
    return jnp.split(z, cuts, axis=-1)


def hgrn_lower_bounds(p):
    p = jax.nn.softmax(p.astype(jnp.float32), axis=0)
    return jnp.cumsum(p, axis=0) - p[0]


def hgrn2_mixer(q_raw, f_raw, i_raw, g_raw, lb, norm_g, w_out):
    B, T, _ = q_raw.shape
    dt = q_raw.dtype
    H, DK, DV, C = HG_HEADS, HG_DK, HG_DV, HG_CHUNK
    f = lb + (1.0 - lb) * jax.nn.sigmoid(f_raw.astype(jnp.float32))
    log_f = jnp.log(f)
    k = 1.0 - f
    q = jax.nn.silu(q_raw.astype(jnp.float32)) * DK ** -0.5
    v = i_raw.astype(jnp.float32)
    pad_front = (-N_META) % C
    pad_back = (-(pad_front + T)) % C
    n_chunks = (pad_front + T + pad_back) // C

    def to_chunks(a, d):
        a = a.reshape(B, T, H, d).transpose(0, 2, 1, 3)
        a = jnp.pad(a, ((0, 0), (0, 0), (pad_front, pad_back), (0, 0)))
        return jnp.moveaxis(a.reshape(B, H, n_chunks, C, d), 2, 0)

    qc, kc, lfc, vc = to_chunks(q, DK), to_chunks(k, DK), to_chunks(log_f, DK), to_chunks(v, DV)
    tri = jnp.tril(jnp.ones((C, C), dtype=bool))

    def step(S, inp):
        q_, k_, lf_, v_ = inp
        b = jnp.cumsum(lf_, axis=2)
        diff = b[:, :, :, None, :] - b[:, :, None, :, :]
        decay = jnp.where(tri[:, :, None], jnp.exp(jnp.minimum(diff, 0.0)), 0.0)
        att = jnp.einsum('bhtd,bhtsd,bhsd->bhts', q_, decay, k_)
        o = (jnp.einsum('bhts,bhsv->bhtv', att, v_)
             + jnp.einsum('bhtd,bhdv->bhtv', q_ * jnp.exp(b), S))
        b_last = b[:, :, -1:, :]
        S = (jnp.exp(b_last[:, :, 0, :])[..., None] * S
             + jnp.einsum('bhsd,bhsv->bhdv', k_ * jnp.exp(b_last - b), v_))
        return S, o

    S0 = jnp.zeros((B, H, DK, DV), jnp.float32)
    _, o = lax.scan(step, S0, (qc, kc, lfc, vc))
    o = jnp.moveaxis(o, 0, 2).reshape(B, H, n_chunks * C, DV)[:, :, pad_front:pad_front + T]
    o = o.transpose(0, 2, 1, 3)
    o = rms_norm(o, norm_g) * jax.nn.silu(g_raw.astype(jnp.float32).reshape(B, T, H, DV))
    return o.reshape(B, T, H * DV).astype(dt) @ w_out


def dsa_mixer(cq, k_raw, v, k_idx, w_idx, cq_norm_g, w_uq, w_qi, q_norm_g, k_norm_g, w_out):
    B, T, _ = cq.shape
    dt = cq.dtype
    n_sel = min(TOPK_MAX, (T - N_META) // 4)
    cqn = rms_norm(cq, cq_norm_g)
    q = rms_norm((cqn @ w_uq).reshape(B, T, AT_HEADS, AT_DH), q_norm_g)
    qi = (cqn @ w_qi).reshape(B, T, IDX_HEADS, IDX_DIM)
    k = rms_norm(k_raw, k_norm_g)
    wts = w_idx * (IDX_HEADS * IDX_DIM) ** -0.5
    scale = AT_DH ** -0.5
    kidx32 = k_idx.astype(jnp.float32)
    kpos = jnp.arange(T, dtype=jnp.int32)
    pad_front = (-N_META) % Q_BLOCK
    pad_back = (-(pad_front + T)) % Q_BLOCK
    Tq = pad_front + T + pad_back
    n_blk = Tq // Q_BLOCK

    def to_blocks(a):
        a = jnp.pad(a, ((0, 0), (pad_front, pad_back)) + ((0, 0),) * (a.ndim - 2))
        return jnp.moveaxis(a.reshape((B, n_blk, Q_BLOCK) + a.shape[2:]), 1, 0)

    qpos = (jnp.arange(Tq, dtype=jnp.int32) - pad_front).reshape(n_blk, Q_BLOCK)

    def block(args):
        qb, qib, wb, pb = args
        s = jax.nn.relu(jnp.einsum('bqhd,bsd->bqhs', qib.astype(jnp.float32), kidx32))
        score = jnp.einsum('bqhs,bqh->bqs', s, wb.astype(jnp.float32))
        causal = kpos[None, :] <= pb[:, None]
        score = jnp.where(causal, jnp.where(kpos[None, :] < N_META, MASK_BIG, score), -MASK_BIG)
        _, idx = lax.top_k(score, n_sel)
        kg = jax.vmap(lambda kb, ib: kb[ib])(k, idx)
        vg = jax.vmap(lambda vb, ib: vb[ib])(v, idx)
        logits = jnp.einsum('bqhd,bqkd->bqhk', qb, kg).astype(jnp.float32) * scale
        valid = idx <= pb[None, :, None]
        logits = jnp.where(valid[:, :, None, :], logits, -MASK_BIG)
        p = jax.nn.softmax(logits, axis=-1).astype(dt)
        return jnp.einsum('bqhk,bqkd->bqhd', p, vg)

    o = lax.map(block, (to_blocks(q), to_blocks(qi), to_blocks(wts), qpos))
    o = jnp.moveaxis(o, 0, 1).reshape(B, Tq, AT_HEADS * AT_DH)[:, pad_front:pad_front + T]
    return o @ w_out


def conformer_conv_mixer(u, dw_w, dw_b, ln_g, ln_b, w_out):
    a, g = jnp.split(u, 2, axis=-1)
    h = a * jax.nn.sigmoid(g)
    h = causal_dwconv(h, dw_w, dw_b)
    h = jax.nn.silu(layer_norm(h, ln_g, ln_b))
    return h @ w_out


def conv_ffn(x, w_up, dw_w, dw_b, w_down):
    h = causal_dwconv(x @ w_up, dw_w, dw_b)
    a, b = jnp.split(h, 2, axis=-1)
    return (jax.nn.silu(a) * b) @ w_down


def setup_inputs(seed: int = 0) -> dict:
    key = jax.random.key(seed)
    ks = jax.random.split(key, 24)
    f32 = jnp.float32
    L = DEPTH

    def w(k, shape, fan_in):
        return jax.random.normal(k, shape, f32) * fan_in ** -0.5

    def gain(k, shape):
        return 1.0 + 0.01 * jax.random.normal(k, shape, f32)

    def bias(k, shape):
        return 0.01 * jax.random.normal(k, shape, f32)

    return {
        'x': jax.random.normal(ks[0], (BATCH, SEQ, D_MODEL), f32),
        'meta_tokens': jax.random.normal(ks[1], (N_META, D_MODEL), f32),
        'hgrn_lb': 0.5 * jax.random.normal(ks[2], (L, HG_HEADS * HG_DK), f32),
        'norm1_g': gain(ks[3], (L, D_MODEL)),
        'w_in': w(ks[4], (L, D_MODEL, N_IN), D_MODEL),
        'hg_norm_g': gain(ks[5], (L, HG_DV)),
        'w_hg_out': w(ks[6], (L, HG_HEADS * HG_DV, D_MODEL), HG_HEADS * HG_DV),
        'cq_norm_g': gain(ks[7], (L, AT_QRANK)),
        'w_uq': w(ks[8], (L, AT_QRANK, AT_HEADS * AT_DH), AT_QRANK),
        'w_qi': w(ks[9], (L, AT_QRANK, IDX_HEADS * IDX_DIM), AT_QRANK),
        'q_norm_g': gain(ks[10], (L, AT_DH)),
        'k_norm_g': gain(ks[11], (L, AT_DH)),
        'w_at_out': w(ks[12], (L, AT_HEADS * AT_DH, D_MODEL), AT_HEADS * AT_DH),
        'cv_dw_w': w(ks[13], (L, CV_WIDTH, CV_CH), CV_WIDTH),
        'cv_dw_b': bias(ks[14], (L, CV_CH)),
        'cv_ln_g': gain(ks[15], (L, CV_CH)),
        'cv_ln_b': bias(ks[16], (L, CV_CH)),
        'w_cv_out': w(ks[17], (L, CV_CH, D_MODEL), CV_CH),
        'w_mix_out': w(ks[18], (L, D_MODEL, D_MODEL), D_MODEL),
        'norm2_g': gain(ks[19], (L, D_MODEL)),
        'w_ffn_up': w(ks[20], (L, D_MODEL, 2 * FF_DIM), D_MODEL),
        'ffn_dw_w': w(ks[21], (L, FF_CONV, 2 * FF_DIM), FF_CONV),
        'ffn_dw_b': bias(ks[22], (L, 2 * FF_DIM)),
        'w_ffn_down': w(ks[23], (L, FF_DIM, D_MODEL), FF_DIM),
    }


def reference(x, meta_tokens, hgrn_lb, norm1_g, w_in, hg_norm_g, w_hg_out, cq_norm_g, w_uq, w_qi,
              q_norm_g, k_norm_g, w_at_out, cv_dw_w, cv_dw_b, cv_ln_g, cv_ln_b, w_cv_out, w_mix_out,
              norm2_g, w_ffn_up, ffn_dw_w, ffn_dw_b, w_ffn_down):
    B = x.shape[0]
    meta = jnp.broadcast_to(meta_tokens.astype(x.dtype)[None], (B, N_META, D_MODEL))
    h = jnp.concatenate([meta, x], axis=1)
    lbs = hgrn_lower_bounds(hgrn_lb)
    for l in range(DEPTH):
        xn = rms_norm(h, norm1_g[l])
        z = xn @ w_in[l]
        (q_hg, f_hg, i_hg, g_hg, cq, k_at, v_at, k_ix, w_ix, u_cv, gate_logits) = _split(z, IN_SPLITS)
        y_hg = hgrn2_mixer(q_hg, f_hg, i_hg, g_hg, lbs[l], hg_norm_g[l], w_hg_out[l])
        y_at = dsa_mixer(cq, k_at, v_at, k_ix, w_ix, cq_norm_g[l], w_uq[l], w_qi[l],
                         q_norm_g[l], k_norm_g[l], w_at_out[l])
        y_cv = conformer_conv_mixer(u_cv, cv_dw_w[l], cv_dw_b[l], cv_ln_g[l], cv_ln_b[l], w_cv_out[l])
        g1, g2, g3 = jnp.split(jax.nn.sigmoid(gate_logits), N_BRANCH, axis=-1)
        h = h + (g1 * y_hg + g2 * y_at + g3 * y_cv) @ w_mix_out[l]
        h = h + conv_ffn(rms_norm(h, norm2_g[l]), w_ffn_up[l], ffn_dw_w[l], ffn_dw_b[l], w_ffn_down[l])
    return h[:, N_META:]
```

```python
import functools

import numpy as np
import jax
import jax.numpy as jnp
from jax import lax
from jax.experimental import pallas as pl
from jax.experimental.pallas import tpu as pltpu

F32 = jnp.float32
BF16 = jnp.bfloat16
I32 = jnp.int32

D_MODEL = 1024
N_META = 16
HG_HEADS = 4
HG_DK = 128
HG_DV = 128
HG_CHUNK = 64
AT_HEADS = 8
AT_DH = 64
AT_QRANK = 256
IDX_HEADS = 4
IDX_DIM = 64
TOPK_MAX = 256
Q_BLOCK = 128
CV_CH = 512
CV_WIDTH = 31
FF_DIM = 2816
FF_CONV = 3
N_BRANCH = 3
EPS = 1e-6
MASK_BIG = 1e30

PADF = Q_BLOCK
DEAD = PADF - N_META
FF_CHUNK = 256
N_FF_CHUNK = FF_DIM // FF_CHUNK
ATT_SLAB = 768
CV_HALO = 32
FF_HALO = 16
VMEM_LIMIT = 56 * 1024 * 1024

_NT = (((1,), (1,)), ((), ()))
_TN = (((0,), (0,)), ((), ()))


def _pick(n, cands):
    for c in cands:
        if n % c == 0:
            return c
    raise ValueError(f"no tile for {n}")


def _const_spec(shape):
    nd = len(shape)
    return pl.BlockSpec(shape, lambda *_: (0,) * nd, pipeline_mode=pl.Buffered(1))


def _params(sem):
    return pltpu.CompilerParams(dimension_semantics=sem, vmem_limit_bytes=VMEM_LIMIT)


def _sigmoid(x):
    return 1.0 / (1.0 + jnp.exp(-x))


def _rms(x, g):
    return x * lax.rsqrt(jnp.mean(x * x, axis=-1, keepdims=True) + EPS) * g


def _in_proj_body(h_ref, g_ref, w_hg, w_at, w_cv, w_gt, o_hg, o_at, o_cv, o_gt):
    xn = _rms(h_ref[...], g_ref[...]).astype(BF16)
    for w, o in ((w_hg, o_hg), (w_at, o_at), (w_cv, o_cv), (w_gt, o_gt)):
        o[...] = jnp.dot(xn, w[...], preferred_element_type=F32).astype(o.dtype)


def _in_proj(h, g, w_hg, w_at, w_cv, w_gt):
    R, D = h.shape
    tm = _pick(R, (256, 128))
    outs = ((w_hg.shape[1], F32), (w_at.shape[1], F32), (w_cv.shape[1], BF16), (w_gt.shape[1], BF16))
    return pl.pallas_call(
        _in_proj_body,
        grid=(R // tm,),
        in_specs=[pl.BlockSpec((tm, D), lambda i: (i, 0)), _const_spec((1, D))]
        + [_const_spec(w.shape) for w in (w_hg, w_at, w_cv, w_gt)],
        out_specs=[pl.BlockSpec((tm, n), lambda i: (i, 0)) for n, _ in outs],
        out_shape=[jax.ShapeDtypeStruct((R, n), dt) for n, dt in outs],
        compiler_params=_params(("parallel",)),
        name="in_proj",
    )(h, g, w_hg, w_at, w_cv, w_gt)


def _hgrn_consts():
    C = HG_CHUNK
    t = np.arange(C)[:, None]
    s = np.arange(C)[None, :]
    mats = [(s <= t)]
    for m in (8, 16, 32):
        rho = (t // (2 * m)) * (2 * m) + m - 1
        mats.append(s <= rho)
    return np.concatenate(mats, axis=0).astype(np.float32)


def _split3(x):
    hi = x.astype(BF16)
    r1 = x - hi.astype(F32)
    mid = r1.astype(BF16)
    lo = (r1 - mid.astype(F32)).astype(BF16)
    return hi, mid, lo


def _hgrn_body(q_ref, f_ref, i_ref, g_ref, lb_ref, ng_ref, cm_ref, o_ref, st_ref, *, tc):
    C = HG_CHUNK
    it = pl.program_id(1)

    @pl.when(it == 0)
    def _():
        st_ref[...] = jnp.zeros_like(st_ref)

    row = lax.broadcasted_iota(I32, (C, C), 0)
    col = lax.broadcasted_iota(I32, (C, C), 1)
    m_diag = (row // 8 == col // 8) & (col <= row)
    level_masks = []
    for m in (8, 16, 32):
        level_masks.append((row // (2 * m) == col // (2 * m)) & ((row // m) % 2 == 1) & ((col // m) % 2 == 0))
    rid = lax.broadcasted_iota(I32, (C, 1), 0)
    second = [(rid // m) % 2 == 1 for m in (8, 16, 32)]
    lane_c = lax.broadcasted_iota(I32, (8, C), 1)
    cm = cm_ref[...]
    ng = ng_ref[...]

    def chunk(c, carry):
        r0 = pl.multiple_of(c * C, C)
        live = (it * tc + r0 + rid) >= DEAD
        for h in range(HG_HEADS):
            cs = slice(h * HG_DK, (h + 1) * HG_DK)
            lb = lb_ref[:, cs]
            f = lb + (1.0 - lb) * _sigmoid(f_ref[pl.ds(r0, C), cs])
            lf = jnp.where(live, jnp.log(f), 0.0)
            k = jnp.where(live, 1.0 - f, 0.0)
            qr = q_ref[pl.ds(r0, C), cs]
            q = qr * _sigmoid(qr) * (HG_DK ** -0.5)
            v = i_ref[pl.ds(r0, C), cs].astype(BF16)
            cums = sum(jnp.dot(cm, part, preferred_element_type=F32) for part in _split3(lf))
            b = cums[0:C]
            att = jnp.zeros((C, C), F32)
            for lvl in range(3):
                r = cums[(lvl + 1) * C:(lvl + 2) * C]
                e = jnp.exp(jnp.minimum(jnp.where(second[lvl], b - r, r - b), 0.0))
                a = lax.dot_general((q * e).astype(BF16), (k * e).astype(BF16), _NT, preferred_element_type=F32)
                att = jnp.where(level_masks[lvl], a, att)
            blocks = []
            for j in range(C // 8):
                bR = b[8 * j:8 * j + 8]
                qR = q[8 * j:8 * j + 8]
                acc = jnp.zeros((8, C), F32)
                for s in range(8 * j, 8 * j + 8):
                    e = jnp.exp(jnp.minimum(bR - b[s:s + 1], 0.0))
                    colv = jnp.sum(qR * (k[s:s + 1] * e), axis=-1, keepdims=True)
                    acc = jnp.where(lane_c == s, colv, acc)
                blocks.append(acc)
            att = jnp.where(m_diag, jnp.concatenate(blocks, axis=0), att)
            st = st_ref[h]
            o = jnp.dot(att.astype(BF16), v, preferred_element_type=F32)
            o = o + lax.dot_general((q * jnp.exp(b)).astype(BF16), st.astype(BF16), _NT, preferred_element_type=F32)
            bl = b[C - 1:C]
            kb = (k * jnp.exp(bl - b)).astype(BF16)
            st_ref[h] = st * jnp.exp(bl) + lax.dot_general(v, kb, _TN, preferred_element_type=F32)
            gr = g_ref[pl.ds(r0, C), cs]
            o_ref[pl.ds(r0, C), cs] = (_rms(o, ng) * (gr * _sigmoid(gr))).astype(BF16)
        return carry

    lax.fori_loop(0, tc // C, chunk, 0)


def _hgrn(zh, lb, ng, B, Tp):
    R = zh.shape[0]
    W = HG_HEADS * HG_DK
    tc = _pick(Tp, (640, 384, 128))
    nt = Tp // tc
    cm = jnp.asarray(_hgrn_consts(), BF16)

    def col(cb):
        return pl.BlockSpec((tc, W), lambda b, t: (b * nt + t, cb))

    return pl.pallas_call(
        functools.partial(_hgrn_body, tc=tc),
        grid=(B, nt),
        in_specs=[col(0), col(1), col(2), col(3), _const_spec((1, W)), _const_spec((1, HG_DV)),
                  _const_spec(cm.shape)],
        out_specs=pl.BlockSpec((tc, W), lambda b, t: (b * nt + t, 0)),
        out_shape=jax.ShapeDtypeStruct((R, W), BF16),
        scratch_shapes=[pltpu.VMEM((HG_HEADS, HG_DV, HG_DK), F32)],
        compiler_params=_params(("parallel", "arbitrary")),
        name="hgrn2",
    )(zh, zh, zh, zh, lb, ng, cm)


def _conv_body(u_ref, uh_ref, dw_ref, db_ref, lg_ref, lb_ref, o_ref, scr, sh_scr, *, tt):
    it = pl.program_id(1)
    RC = 32
    L = tt + CV_HALO - 8

    def glu(u, pos0):
        u = u.astype(F32)
        hh = u[:, :CV_CH] * _sigmoid(u[:, CV_CH:])
        pos = pos0 + lax.broadcasted_iota(I32, (u.shape[0], 1), 0)
        return jnp.where(pos >= DEAD, hh, 0.0)

    scr[0:CV_HALO] = glu(uh_ref[...], it * tt - CV_HALO)
    scr[CV_HALO:CV_HALO + tt] = glu(u_ref[...], it * tt)
    for s in range(1, 8):
        sh_scr[s - 1] = scr[s:s + L]
    bias = db_ref[...]
    lg = lg_ref[...]
    lbias = lb_ref[...]

    def chunk(c, carry):
        r0 = pl.multiple_of(c * RC, RC)
        acc = jnp.zeros((RC, CV_CH), F32) + bias
        for j in range(CV_WIDTH):
            off = CV_HALO - CV_WIDTH + 1 + j
            base = pl.ds(r0 + (off // 8) * 8, RC)
            xs = scr[base, :] if off % 8 == 0 else sh_scr[off % 8 - 1, base, :]
            acc = acc + dw_ref[j:j + 1, :] * xs
        mu = jnp.mean(acc, axis=-1, keepdims=True)
        xc = acc - mu
        y = xc * lax.rsqrt(jnp.mean(xc * xc, axis=-1, keepdims=True) + EPS) * lg + lbias
        o_ref[pl.ds(r0, RC), :] = (y * _sigmoid(y)).astype(BF16)
        return carry

    lax.fori_loop(0, tt // RC, chunk, 0)


def _conv_branch(zu, dw, db, lg, lb, B, Tp):
    R = zu.shape[0]
    tt = _pick(Tp, (640, 384, 128))
    nt = Tp // tt
    hb = tt // CV_HALO
    return pl.pallas_call(
        functools.partial(_conv_body, tt=tt),
        grid=(B, nt),
        in_specs=[pl.BlockSpec((tt, 2 * CV_CH), lambda b, t: (b * nt + t, 0)),
                  pl.BlockSpec((CV_HALO, 2 * CV_CH), lambda b, t: (jnp.maximum((b * nt + t) * hb - 1, 0), 0)),
                  _const_spec(dw.shape), _const_spec((1, CV_CH)), _const_spec((1, CV_CH)), _const_spec((1, CV_CH))],
        out_specs=pl.BlockSpec((tt, CV_CH), lambda b, t: (b * nt + t, 0)),
        out_shape=jax.ShapeDtypeStruct((R, CV_CH), BF16),
        scratch_shapes=[pltpu.VMEM((CV_HALO + tt, CV_CH), F32), pltpu.VMEM((7, CV_HALO + tt - 8, CV_CH), F32)],
        compiler_params=_params(("parallel", "parallel")),
        name="conformer_conv",
    )(zu, zu, dw, db, lg, lb)


def _dsa_prep_body(z_ref, cg_ref, wuq_ref, wqi_ref, qg_ref, kg_ref, q_ref, qi_ref, k_ref, v_ref, ki_ref, w_ref):
    z = z_ref[...]
    cqn = _rms(z[:, 0:AT_QRANK], cg_ref[...]).astype(BF16)
    qg = qg_ref[...]
    for h in range(AT_HEADS):
        qh = jnp.dot(cqn, wuq_ref[h], preferred_element_type=F32)
        q_ref[h] = (_rms(qh, qg) * (AT_DH ** -0.5)).astype(BF16)
    for h in range(IDX_HEADS):
        qi_ref[h] = jnp.dot(cqn, wqi_ref[h], preferred_element_type=F32).astype(BF16)
    k_ref[...] = _rms(z[:, 256:256 + AT_DH], kg_ref[...]).astype(BF16)
    v_ref[...] = z[:, 384:384 + AT_DH].astype(BF16)
    ki_ref[...] = z[:, 512:512 + IDX_DIM].astype(BF16)
    w_ref[...] = z[:, 640:768] * ((IDX_HEADS * IDX_DIM) ** -0.5)


def _dsa_prep(zatt, cg, wuq, wqi, qg, kg):
    R = zatt.shape[0]
    tm = _pick(R, (512, 256, 128))
    row64 = pl.BlockSpec((tm, 64), lambda i: (i, 0))
    return pl.pallas_call(
        _dsa_prep_body,
        grid=(R // tm,),
        in_specs=[pl.BlockSpec((tm, ATT_SLAB), lambda i: (i, 0)), _const_spec((1, AT_QRANK)),
                  _const_spec(wuq.shape), _const_spec(wqi.shape), _const_spec((1, AT_DH)), _const_spec((1, AT_DH))],
        out_specs=[pl.BlockSpec((AT_HEADS, tm, AT_DH), lambda i: (0, i, 0)),
                   pl.BlockSpec((IDX_HEADS, tm, IDX_DIM), lambda i: (0, i, 0)),
                   row64, row64, row64, pl.BlockSpec((tm, 128), lambda i: (i, 0))],
        out_shape=[jax.ShapeDtypeStruct((AT_HEADS, R, AT_DH), BF16),
                   jax.ShapeDtypeStruct((IDX_HEADS, R, IDX_DIM), BF16),
                   jax.ShapeDtypeStruct((R, AT_DH), BF16), jax.ShapeDtypeStruct((R, AT_DH), BF16),
                   jax.ShapeDtypeStruct((R, IDX_DIM), BF16), jax.ShapeDtypeStruct((R, 128), F32)],
        compiler_params=_params(("parallel",)),
        name="dsa_prep",
    )(zatt, cg, wuq, wqi, qg, kg)


def _dsa_body(q_ref, qi_ref, w_ref, k_ref, v_ref, ki_ref, wo_ref, tri_ref, o_ref,
              sc_ref, m_ref, l_ref, acc_ref, *, n_sel):
    QB = Q_BLOCK
    qb = pl.program_id(1)
    nkb = qb + 1
    rowi = lax.broadcasted_iota(I32, (QB, QB), 0)
    lanei = lax.broadcasted_iota(I32, (QB, QB), 1)
    qpos = qb * QB + rowi

    qis = [qi_ref[h] for h in range(IDX_HEADS)]
    wv = w_ref[...]
    wcols = [wv[:, h:h + 1] for h in range(IDX_HEADS)]

    def score_blk(j, carry):
        kij = ki_ref[pl.ds(pl.multiple_of(j * QB, QB), QB), :]
        tot = jnp.zeros((QB, QB), F32)
        for h in range(IDX_HEADS):
            s = lax.dot_general(qis[h], kij, _NT, preferred_element_type=F32)
            tot = tot + jnp.maximum(s, 0.0) * wcols[h]
        kpos = j * QB + lanei
        tot = jnp.where(kpos < PADF, MASK_BIG, tot)
        tot = jnp.where((kpos <= qpos) & (kpos >= DEAD), tot, -MASK_BIG)
        bits = pltpu.bitcast(tot + 0.0, I32)
        sc_ref[j] = bits ^ ((bits >> 31) & 0x7FFFFFFF)
        return carry

    lax.fori_loop(0, nkb, score_blk, 0)

    def count(pred):
        def cb(j, a):
            return a + jnp.where(pred(sc_ref[j]), 1.0, 0.0)
        a = lax.fori_loop(0, nkb, cb, jnp.zeros((QB, QB), F32))
        return jnp.sum(a, axis=-1, keepdims=True)

    def bit_body(bi, c):
        trial = c ^ lax.shift_left(jnp.int32(1), 31 - bi)
        cnt = count(lambda key: key >= trial)
        return jnp.where(cnt >= n_sel, trial, c)

    c = lax.fori_loop(0, 32, bit_body, jnp.full((QB, 1), -2 ** 31, I32))
    need = n_sel - count(lambda key: key > c)

    Q = q_ref[...].reshape(AT_HEADS * QB, AT_DH)
    m_ref[...] = jnp.full(m_ref.shape, -MASK_BIG, F32)
    l_ref[...] = jnp.zeros(l_ref.shape, F32)
    acc_ref[...] = jnp.zeros(acc_ref.shape, F32)
    tri = tri_ref[...]

    def att_blk(j, eqc):
        r0 = pl.multiple_of(j * QB, QB)
        kj = k_ref[pl.ds(r0, QB), :]
        vj = v_ref[pl.ds(r0, QB), :]
        key = sc_ref[j]
        eq = key == c
        eqf = jnp.where(eq, 1.0, 0.0)
        pref = jnp.dot(eqf.astype(BF16), tri, preferred_element_type=F32) + eqc
        kpos = j * QB + lanei
        sel = ((key > c) | (eq & (pref <= need))) & (kpos <= qpos) & (kpos >= DEAD)
        s = lax.dot_general(Q, kj, _NT, preferred_element_type=F32)
        s = jnp.where(sel[None], s.reshape(AT_HEADS, QB, QB), -MASK_BIG).reshape(AT_HEADS * QB, QB)
        m_old = m_ref[...]
        m_new = jnp.maximum(m_old, jnp.max(s, axis=-1, keepdims=True))
        p = jnp.exp(s - m_new)
        alpha = jnp.exp(m_old - m_new)
        l_ref[...] = alpha * l_ref[...] + jnp.sum(p, axis=-1, keepdims=True)
        acc_ref[...] = alpha * acc_ref[...] + jnp.dot(p.astype(BF16), vj, preferred_element_type=F32)
        m_ref[...] = m_new
        return eqc + jnp.sum(eqf, axis=-1, keepdims=True)

    lax.fori_loop(0, nkb, att_blk, jnp.zeros((QB, 1), F32))

    o = acc_ref[...] / l_ref[...]
    y = jnp.zeros((QB, D_MODEL), F32)
    for h in range(AT_HEADS):
        y = y + jnp.dot(o[h * QB:(h + 1) * QB].astype(BF16), wo_ref[h], preferred_element_type=F32)
    o_ref[...] = y.astype(BF16)


def _dsa_main(q, qi, wts, k, v, ki, wo, B, Tp, n_sel):
    R = k.shape[0]
    QB = Q_BLOCK
    nb = Tp // QB
    tri = jnp.asarray(np.triu(np.ones((QB, QB), np.float32)), BF16)
    seq = lambda: pl.BlockSpec((Tp, 64), lambda b, i: (b, 0))
    return pl.pallas_call(
        functools.partial(_dsa_body, n_sel=n_sel),
        grid=(B, nb),
        in_specs=[pl.BlockSpec((AT_HEADS, QB, AT_DH), lambda b, i: (0, b * nb + i, 0)),
                  pl.BlockSpec((IDX_HEADS, QB, IDX_DIM), lambda b, i: (0, b * nb + i, 0)),
                  pl.BlockSpec((QB, 128), lambda b, i: (b * nb + i, 0)),
                  seq(), seq(), seq(), _const_spec(wo.shape), _const_spec(tri.shape)],
        out_specs=pl.BlockSpec((QB, D_MODEL), lambda b, i: (b * nb + i, 0)),
        out_shape=jax.ShapeDtypeStruct((R, D_MODEL), BF16),
        scratch_shapes=[pltpu.VMEM((nb, QB, QB), I32),
                        pltpu.VMEM((AT_HEADS * QB, 1), F32), pltpu.VMEM((AT_HEADS * QB, 1), F32),
                        pltpu.VMEM((AT_HEADS * QB, AT_DH), F32)],
        compiler_params=_params(("parallel", "arbitrary")),
        name="dsa_attention",
    )(q, qi, wts, k, v, ki, wo, tri)


def _mix_body(ohg_ref, yat_ref, ocv_ref, gt_ref, h_ref, whg_ref, wcv_ref, wmix_ref, o_ref):
    y_hg = jnp.dot(ohg_ref[...], whg_ref[...], preferred_element_type=F32)
    y_cv = jnp.dot(ocv_ref[...], wcv_ref[...], preferred_element_type=F32)
    y_at = yat_ref[...].astype(F32)
    g1 = _sigmoid(gt_ref[:, 0:D_MODEL].astype(F32))
    g2 = _sigmoid(gt_ref[:, D_MODEL:2 * D_MODEL].astype(F32))
    g3 = _sigmoid(gt_ref[:, 2 * D_MODEL:3 * D_MODEL].astype(F32))
    mixed = (g1 * y_hg + g2 * y_at + g3 * y_cv).astype(BF16)
    o_ref[...] = h_ref[...] + jnp.dot(mixed, wmix_ref[...], preferred_element_type=F32)


def _mix(ohg, yat, ocv, gates, h, whg, wcv, wmix):
    R, D = h.shape
    tm = _pick(R, (512, 256, 128))
    rows = lambda n: pl.BlockSpec((tm, n), lambda i: (i, 0))
    return pl.pallas_call(
        _mix_body,
        grid=(R // tm,),
        in_specs=[rows(ohg.shape[1]), rows(D), rows(ocv.shape[1]), rows(3 * D), rows(D),
                  _const_spec(whg.shape), _const_spec(wcv.shape), _const_spec(wmix.shape)],
        out_specs=rows(D),
        out_shape=jax.ShapeDtypeStruct((R, D), F32),
        compiler_params=_params(("parallel",)),
        name="branch_mix",
    )(ohg, yat, ocv, gates, h, whg, wcv, wmix)


def _ffn_body(h_ref, hh_ref, g_ref, wa_ref, wb_ref, cwa_ref, cwb_ref, cba_ref, cbb_ref, wd_ref, o_ref,
              xn_scr, ua_scr, ub_scr, acc_scr, *, tm):
    it = pl.program_id(1)
    g = g_ref[...]
    xn_scr[0:FF_HALO] = _rms(hh_ref[...], g).astype(BF16)
    xn_scr[FF_HALO:FF_HALO + tm] = _rms(h_ref[...], g).astype(BF16)
    pos = it * tm - FF_HALO + lax.broadcasted_iota(I32, (tm + FF_HALO, 1), 0)
    live = pos >= DEAD
    acc_scr[...] = jnp.zeros(acc_scr.shape, F32)

    def conv(scr, cw, cb):
        y = cb
        for j in range(FF_CONV):
            y = y + cw[j:j + 1] * scr[FF_HALO - (FF_CONV - 1) + j:FF_HALO - (FF_CONV - 1) + j + tm]
        return y

    def chunk(c, carry):
        xn = xn_scr[...]
        ua_scr[...] = jnp.where(live, jnp.dot(xn, wa_ref[c], preferred_element_type=F32), 0.0)
        ub_scr[...] = jnp.where(live, jnp.dot(xn, wb_ref[c], preferred_element_type=F32), 0.0)
        a = conv(ua_scr, cwa_ref[c], cba_ref[c])
        b = conv(ub_scr, cwb_ref[c], cbb_ref[c])
        act = (a * _sigmoid(a) * b).astype(BF16)
        acc_scr[...] += jnp.dot(act, wd_ref[c], preferred_element_type=F32)
        return carry

    lax.fori_loop(0, N_FF_CHUNK, chunk, 0)
    o_ref[...] = h_ref[...] + acc_scr[...]


def _ffn(h, g, wa, wb, cwa, cwb, cba, cbb, wd, B, Tp):
    R, D = h.shape
    tm = _pick(Tp, (640, 384, 128))
    nt = Tp // tm
    hb = tm // FF_HALO
    return pl.pallas_call(
        functools.partial(_ffn_body, tm=tm),
        grid=(B, nt),
        in_specs=[pl.BlockSpec((tm, D), lambda b, t: (b * nt + t, 0)),
                  pl.BlockSpec((FF_HALO, D), lambda b, t: (jnp.maximum((b * nt + t) * hb - 1, 0), 0)),
                  _const_spec((1, D))] + [_const_spec(w.shape) for w in (wa, wb, cwa, cwb, cba, cbb, wd)],
        out_specs=pl.BlockSpec((tm, D), lambda b, t: (b * nt + t, 0)),
        out_shape=jax.ShapeDtypeStruct((R, D), F32),
        scratch_shapes=[pltpu.VMEM((FF_HALO + tm, D), BF16),
                        pltpu.VMEM((FF_HALO + tm, FF_CHUNK), F32), pltpu.VMEM((FF_HALO + tm, FF_CHUNK), F32),
                        pltpu.VMEM((tm, D), F32)],
        compiler_params=_params(("parallel", "parallel")),
        name="conv_ffn",
    )(h, h, g, wa, wb, cwa, cwb, cba, cbb, wd)


def _pack_in_proj(w):
    HW = HG_HEADS * HG_DK
    o = 4 * HW
    cq = w[:, o:o + AT_QRANK]
    o += AT_QRANK
    pieces = [cq]
    for n in (AT_DH, AT_DH, IDX_DIM, IDX_HEADS):
        pieces.append(jnp.pad(w[:, o:o + n], ((0, 0), (0, 128 - n))))
        o += n
    w_at = jnp.concatenate(pieces, axis=1)
    w_cv = w[:, o:o + 2 * CV_CH]
    o += 2 * CV_CH
    w_gt = w[:, o:o + N_BRANCH * D_MODEL]
    return [a.astype(BF16) for a in (w[:, :4 * HW], w_at, w_cv, w_gt)]


def _chunk_cols(w):
    return w.reshape(w.shape[0], N_FF_CHUNK, FF_CHUNK).transpose(1, 0, 2)


def kernel(x, meta_tokens, hgrn_lb, norm1_g, w_in, hg_norm_g, w_hg_out, cq_norm_g, w_uq, w_qi, q_norm_g, k_norm_g, w_at_out, cv_dw_w, cv_dw_b, cv_ln_g, cv_ln_b, w_cv_out, w_mix_out, norm2_g, w_ffn_up, ffn_dw_w, ffn_dw_b, w_ffn_down):
    B, SEQ, D = x.shape
    depth = w_in.shape[0]
    Tp = PADF + SEQ
    R = B * Tp
    n_sel = min(TOPK_MAX, SEQ // 4)

    meta = jnp.broadcast_to(meta_tokens.astype(x.dtype)[None], (B, N_META, D))
    h = jnp.concatenate([jnp.zeros((B, DEAD, D), x.dtype), meta, x], axis=1).reshape(R, D)
    p = jax.nn.softmax(hgrn_lb.astype(F32), axis=0)
    lbs = jnp.cumsum(p, axis=0) - p[0]

    row = lambda a: a.reshape(1, -1).astype(F32)
    for l in range(depth):
        w_hg, w_at, w_cv, w_gt = _pack_in_proj(w_in[l])
        zh, zatt, zu, zg = _in_proj(h, row(norm1_g[l]), w_hg, w_at, w_cv, w_gt)

        o_hg = _hgrn(zh, row(lbs[l]), row(hg_norm_g[l]), B, Tp)
        o_cv = _conv_branch(zu, cv_dw_w[l].astype(F32), row(cv_dw_b[l]), row(cv_ln_g[l]), row(cv_ln_b[l]), B, Tp)

        wuq = w_uq[l].reshape(AT_QRANK, AT_HEADS, AT_DH).transpose(1, 0, 2).astype(BF16)
        wqi = w_qi[l].reshape(AT_QRANK, IDX_HEADS, IDX_DIM).transpose(1, 0, 2).astype(BF16)
        q, qi, k, v, ki, wts = _dsa_prep(zatt, row(cq_norm_g[l]), wuq, wqi, row(q_norm_g[l]), row(k_norm_g[l]))
        wo = w_at_out[l].reshape(AT_HEADS, AT_DH, D).astype(BF16)
        y_at = _dsa_main(q, qi, wts, k, v, ki, wo, B, Tp, n_sel)

        h = _mix(o_hg, y_at, o_cv, zg, h, w_hg_out[l].astype(BF16), w_cv_out[l].astype(BF16),
                 w_mix_out[l].astype(BF16))

        wa = _chunk_cols(w_ffn_up[l][:, :FF_DIM]).astype(BF16)
        wb = _chunk_cols(w_ffn_up[l][:, FF_DIM:]).astype(BF16)
        cwa = _chunk_cols(ffn_dw_w[l][:, :FF_DIM]).astype(F32)
        cwb = _chunk_cols(ffn_dw_w[l][:, FF_DIM:]).astype(F32)
        cba = _chunk_cols(ffn_dw_b[l][None, :FF_DIM]).astype(F32)
        cbb = _chunk_cols(ffn_dw_b[l][None, FF_DIM:]).astype(F32)
        wd = w_ffn_down[l].reshape(N_FF_CHUNK, FF_CHUNK, D).astype(BF16)
        h = _ffn(h, row(norm2_g[l]), wa, wb, cwa, cwb, cba, cbb, wd, B, Tp)

    return h.reshape(B, Tp, D)[:, PADF:]
```

```python
import functools

import numpy as np
import jax
import jax.numpy as jnp
from jax import lax
from jax.experimental import pallas as pl
from jax.experimental.pallas import tpu as pltpu

F32 = jnp.float32
BF16 = jnp.bfloat16
I32 = jnp.int32

D_MODEL = 1024
N_META = 16
HG_HEADS = 4
HG_DK = 128
HG_DV = 128
HG_CHUNK = 64
AT_HEADS = 8
AT_DH = 64
AT_QRANK = 256
IDX_HEADS = 4
IDX_DIM = 64
TOPK_MAX = 256
Q_BLOCK = 128
CV_CH = 512
CV_WIDTH = 31
FF_DIM = 2816
FF_CONV = 3
N_BRANCH = 3
EPS = 1e-6
MASK_BIG = 1e30

PADF = Q_BLOCK
DEAD = PADF - N_META
FF_CHUNK = 256
N_FF_CHUNK = FF_DIM // FF_CHUNK
ATT_SLAB = 768
CV_HALO = 32
FF_HALO = 16
DSA_GROUP = 4
LOG2E = 1.4426950408889634
VT_ROWS = AT_DH + 16
KEY_NEG = int(np.array(-MASK_BIG, np.float32).view(np.int32)) ^ 0x7FFFFFFF
VMEM_LIMIT = 56 * 1024 * 1024

_NT = (((1,), (1,)), ((), ()))
_TN = (((0,), (0,)), ((), ()))


def _pick(n, cands):
    for c in cands:
        if n % c == 0:
            return c
    raise ValueError(f"no tile for {n}")


def _const_spec(shape):
    nd = len(shape)
    return pl.BlockSpec(shape, lambda *_: (0,) * nd, pipeline_mode=pl.Buffered(1))


def _params(sem):
    return pltpu.CompilerParams(dimension_semantics=sem, vmem_limit_bytes=VMEM_LIMIT)


def _sigmoid(x):
    return 1.0 / (1.0 + jnp.exp(-x))


def _rms(x, g):
    return x * lax.rsqrt(jnp.mean(x * x, axis=-1, keepdims=True) + EPS) * g


def _in_proj_body(h_ref, g_ref, w_hg, w_at, w_cv, w_gt, o_hg, o_at, o_cv, o_gt):
    xn = _rms(h_ref[...], g_ref[...]).astype(BF16)
    for w, o in ((w_hg, o_hg), (w_at, o_at), (w_cv, o_cv), (w_gt, o_gt)):
        o[...] = jnp.dot(xn, w[...], preferred_element_type=F32).astype(o.dtype)


def _in_proj(h, g, w_hg, w_at, w_cv, w_gt):
    R, D = h.shape
    tm = _pick(R, (256, 128))
    outs = ((w_hg.shape[1], F32), (w_at.shape[1], F32), (w_cv.shape[1], BF16), (w_gt.shape[1], BF16))
    return pl.pallas_call(
        _in_proj_body,
        grid=(R // tm,),
        in_specs=[pl.BlockSpec((tm, D), lambda i: (i, 0)), _const_spec((1, D))]
        + [_const_spec(w.shape) for w in (w_hg, w_at, w_cv, w_gt)],
        out_specs=[pl.BlockSpec((tm, n), lambda i: (i, 0)) for n, _ in outs],
        out_shape=[jax.ShapeDtypeStruct((R, n), dt) for n, dt in outs],
        compiler_params=_params(("parallel",)),
        name="in_proj",
    )(h, g, w_hg, w_at, w_cv, w_gt)


def _hgrn_consts():
    C = HG_CHUNK
    t = np.arange(C)[:, None]
    s = np.arange(C)[None, :]
    mats = [(s <= t)]
    for m in (8, 16, 32):
        rho = (t // (2 * m)) * (2 * m) + m - 1
        mats.append(s <= rho)
    return np.concatenate(mats, axis=0).astype(np.float32)


def _split3(x):
    hi = x.astype(BF16)
    r1 = x - hi.astype(F32)
    mid = r1.astype(BF16)
    lo = (r1 - mid.astype(F32)).astype(BF16)
    return hi, mid, lo


def _hgrn_body(q_ref, f_ref, i_ref, g_ref, lb_ref, ng_ref, cm_ref, o_ref, st_ref, *, tc):
    C = HG_CHUNK
    it = pl.program_id(1)

    @pl.when(it == 0)
    def _():
        st_ref[...] = jnp.zeros_like(st_ref)

    row = lax.broadcasted_iota(I32, (C, C), 0)
    col = lax.broadcasted_iota(I32, (C, C), 1)
    m_diag = (row // 8 == col // 8) & (col <= row)
    level_masks = []
    for m in (8, 16, 32):
        level_masks.append((row // (2 * m) == col // (2 * m)) & ((row // m) % 2 == 1) & ((col // m) % 2 == 0))
    rid = lax.broadcasted_iota(I32, (C, 1), 0)
    second = [(rid // m) % 2 == 1 for m in (8, 16, 32)]
    lane_c = lax.broadcasted_iota(I32, (8, C), 1)
    cm = cm_ref[...]
    ng = ng_ref[...]

    def chunk(c, carry):
        r0 = pl.multiple_of(c * C, C)
        live = (it * tc + r0 + rid) >= DEAD
        for h in range(HG_HEADS):
            cs = slice(h * HG_DK, (h + 1) * HG_DK)
            lb = lb_ref[:, cs]
            f = lb + (1.0 - lb) * _sigmoid(f_ref[pl.ds(r0, C), cs])
            lf = jnp.where(live, jnp.log(f), 0.0)
            k = jnp.where(live, 1.0 - f, 0.0)
            qr = q_ref[pl.ds(r0, C), cs]
            q = qr * _sigmoid(qr) * (HG_DK ** -0.5)
            v = i_ref[pl.ds(r0, C), cs].astype(BF16)
            cums = sum(jnp.dot(cm, part, preferred_element_type=F32) for part in _split3(lf))
            b = cums[0:C]
            att = jnp.zeros((C, C), F32)
            for lvl in range(3):
                r = cums[(lvl + 1) * C:(lvl + 2) * C]
                e = jnp.exp(jnp.minimum(jnp.where(second[lvl], b - r, r - b), 0.0))
                a = lax.dot_general((q * e).astype(BF16), (k * e).astype(BF16), _NT, preferred_element_type=F32)
                att = jnp.where(level_masks[lvl], a, att)
            blocks = []
            for j in range(C // 8):
                bR = b[8 * j:8 * j + 8]
                qR = q[8 * j:8 * j + 8]
                acc = jnp.zeros((8, C), F32)
                for s in range(8 * j, 8 * j + 8):
                    e = jnp.exp(jnp.minimum(bR - b[s:s + 1], 0.0))
                    colv = jnp.sum(qR * (k[s:s + 1] * e), axis=-1, keepdims=True)
                    acc = jnp.where(lane_c == s, colv, acc)
                blocks.append(acc)
            att = jnp.where(m_diag, jnp.concatenate(blocks, axis=0), att)
            st = st_ref[h]
            o = jnp.dot(att.astype(BF16), v, preferred_element_type=F32)
            o = o + lax.dot_general((q * jnp.exp(b)).astype(BF16), st.astype(BF16), _NT, preferred_element_type=F32)
            bl = b[C - 1:C]
            kb = (k * jnp.exp(bl - b)).astype(BF16)
            st_ref[h] = st * jnp.exp(bl) + lax.dot_general(v, kb, _TN, preferred_element_type=F32)
            gr = g_ref[pl.ds(r0, C), cs]
            o_ref[pl.ds(r0, C), cs] = (_rms(o, ng) * (gr * _sigmoid(gr))).astype(BF16)
        return carry

    lax.fori_loop(0, tc // C, chunk, 0)


def _hgrn(zh, lb, ng, B, Tp):
    R = zh.shape[0]
    W = HG_HEADS * HG_DK
    tc = _pick(Tp, (640, 384, 128))
    nt = Tp // tc
    cm = jnp.asarray(_hgrn_consts(), BF16)

    def col(cb):
        return pl.BlockSpec((tc, W), lambda b, t: (b * nt + t, cb))

    return pl.pallas_call(
        functools.partial(_hgrn_body, tc=tc),
        grid=(B, nt),
        in_specs=[col(0), col(1), col(2), col(3), _const_spec((1, W)), _const_spec((1, HG_DV)),
                  _const_spec(cm.shape)],
        out_specs=pl.BlockSpec((tc, W), lambda b, t: (b * nt + t, 0)),
        out_shape=jax.ShapeDtypeStruct((R, W), BF16),
        scratch_shapes=[pltpu.VMEM((HG_HEADS, HG_DV, HG_DK), F32)],
        compiler_params=_params(("parallel", "arbitrary")),
        name="hgrn2",
    )(zh, zh, zh, zh, lb, ng, cm)


def _conv_body(u_ref, uh_ref, dw_ref, db_ref, lg_ref, lb_ref, o_ref, scr, sh_scr, *, tt):
    it = pl.program_id(1)
    RC = 32
    L = tt + CV_HALO - 8

    def glu(u, pos0):
        u = u.astype(F32)
        hh = u[:, :CV_CH] * _sigmoid(u[:, CV_CH:])
        pos = pos0 + lax.broadcasted_iota(I32, (u.shape[0], 1), 0)
        return jnp.where(pos >= DEAD, hh, 0.0)

    scr[0:CV_HALO] = glu(uh_ref[...], it * tt - CV_HALO)
    scr[CV_HALO:CV_HALO + tt] = glu(u_ref[...], it * tt)
    for s in range(1, 8):
        sh_scr[s - 1] = scr[s:s + L]
    bias = db_ref[...]
    lg = lg_ref[...]
    lbias = lb_ref[...]

    def chunk(c, carry):
        r0 = pl.multiple_of(c * RC, RC)
        acc = jnp.zeros((RC, CV_CH), F32) + bias
        for j in range(CV_WIDTH):
            off = CV_HALO - CV_WIDTH + 1 + j
            base = pl.ds(r0 + (off // 8) * 8, RC)
            xs = scr[base, :] if off % 8 == 0 else sh_scr[off % 8 - 1, base, :]
            acc = acc + dw_ref[j:j + 1, :] * xs
        mu = jnp.mean(acc, axis=-1, keepdims=True)
        xc = acc - mu
        y = xc * lax.rsqrt(jnp.mean(xc * xc, axis=-1, keepdims=True) + EPS) * lg + lbias
        o_ref[pl.ds(r0, RC), :] = (y * _sigmoid(y)).astype(BF16)
        return carry

    lax.fori_loop(0, tt // RC, chunk, 0)


def _conv_branch(zu, dw, db, lg, lb, B, Tp):
    R = zu.shape[0]
    tt = _pick(Tp, (640, 384, 128))
    nt = Tp // tt
    hb = tt // CV_HALO
    return pl.pallas_call(
        functools.partial(_conv_body, tt=tt),
        grid=(B, nt),
        in_specs=[pl.BlockSpec((tt, 2 * CV_CH), lambda b, t: (b * nt + t, 0)),
                  pl.BlockSpec((CV_HALO, 2 * CV_CH), lambda b, t: (jnp.maximum((b * nt + t) * hb - 1, 0), 0)),
                  _const_spec(dw.shape), _const_spec((1, CV_CH)), _const_spec((1, CV_CH)), _const_spec((1, CV_CH))],
        out_specs=pl.BlockSpec((tt, CV_CH), lambda b, t: (b * nt + t, 0)),
        out_shape=jax.ShapeDtypeStruct((R, CV_CH), BF16),
        scratch_shapes=[pltpu.VMEM((CV_HALO + tt, CV_CH), F32), pltpu.VMEM((7, CV_HALO + tt - 8, CV_CH), F32)],
        compiler_params=_params(("parallel", "parallel")),
        name="conformer_conv",
    )(zu, zu, dw, db, lg, lb)


def _dsa_prep_body(z_ref, cg_ref, wuq_ref, wqi_ref, qg_ref, kg_ref, eye_ref,
                   q_ref, qi_ref, k_ref, vt_ref, ki_ref, w_ref):
    z = z_ref[...]
    cqn = _rms(z[:, 0:AT_QRANK], cg_ref[...]).astype(BF16)
    qg = qg_ref[...]
    for h in range(AT_HEADS):
        qh = lax.dot_general(wuq_ref[h], cqn, _NT, preferred_element_type=F32)
        qh = qh * lax.rsqrt(jnp.mean(qh * qh, axis=0, keepdims=True) + EPS) * qg
        q_ref[h] = (qh * (AT_DH ** -0.5 * LOG2E)).astype(BF16)
    for h in range(IDX_HEADS):
        qi_ref[h] = lax.dot_general(wqi_ref[h], cqn, _NT, preferred_element_type=F32).astype(BF16)
    k_ref[...] = _rms(z[:, 256:256 + AT_DH], kg_ref[...]).astype(BF16)
    vt_ref[...] = lax.dot_general(eye_ref[...], z[:, 384:384 + AT_DH].astype(BF16), _NT,
                                  preferred_element_type=F32).astype(BF16)
    ki_ref[...] = z[:, 512:512 + IDX_DIM].astype(BF16)
    w_ref[...] = z[:, 640:768] * ((IDX_HEADS * IDX_DIM) ** -0.5)


def _dsa_prep(zatt, cg, wuq_t, wqi_t, qg_col, kg):
    R = zatt.shape[0]
    tm = _pick(R, (512, 256, 128))
    row64 = pl.BlockSpec((tm, 64), lambda i: (i, 0))
    eye = jnp.eye(AT_DH, dtype=BF16)
    return pl.pallas_call(
        _dsa_prep_body,
        grid=(R // tm,),
        in_specs=[pl.BlockSpec((tm, ATT_SLAB), lambda i: (i, 0)), _const_spec((1, AT_QRANK)),
                  _const_spec(wuq_t.shape), _const_spec(wqi_t.shape), _const_spec((AT_DH, 1)),
                  _const_spec((1, AT_DH)), _const_spec(eye.shape)],
        out_specs=[pl.BlockSpec((AT_HEADS, AT_DH, tm), lambda i: (0, 0, i)),
                   pl.BlockSpec((IDX_HEADS, IDX_DIM, tm), lambda i: (0, 0, i)),
                   row64, pl.BlockSpec((AT_DH, tm), lambda i: (0, i)), row64,
                   pl.BlockSpec((tm, 128), lambda i: (i, 0))],
        out_shape=[jax.ShapeDtypeStruct((AT_HEADS, AT_DH, R), BF16),
                   jax.ShapeDtypeStruct((IDX_HEADS, IDX_DIM, R), BF16),
                   jax.ShapeDtypeStruct((R, AT_DH), BF16), jax.ShapeDtypeStruct((AT_DH, R), BF16),
                   jax.ShapeDtypeStruct((R, IDX_DIM), BF16), jax.ShapeDtypeStruct((R, 128), F32)],
        compiler_params=_params(("parallel",)),
        name="dsa_prep",
    )(zatt, cg, wuq_t, wqi_t, qg_col, kg, eye)


def _dsa_body(qt_ref, qit_ref, wt_ref, k_ref, vt_ref, ki_ref, wo_ref, tri_ref, o_ref,
              sc_ref, hi_ref, lo_ref, bias_ref, s_ref, p_ref, m_ref, acc_ref, *, n_sel):
    QB = Q_BLOCK
    G = DSA_GROUP
    I16 = jnp.int16
    qb = pl.program_id(1)
    ngrp = (qb + G) // G
    keyi = lax.broadcasted_iota(I32, (QB, QB), 0)
    qpos = qb * QB + lax.broadcasted_iota(I32, (QB, QB), 1)

    qits = [qit_ref[h] for h in range(IDX_HEADS)]
    wrows = [wt_ref[h:h + 1, :] for h in range(IDX_HEADS)]

    def score_grp(g, masked):
        for u in range(G):
            j = g * G + u
            kij = ki_ref[pl.ds(pl.multiple_of(j * QB, QB), QB), :]
            tot = jnp.zeros((QB, QB), F32)
            for h in range(IDX_HEADS):
                s = jnp.dot(kij, qits[h], preferred_element_type=F32)
                tot = tot + jnp.maximum(s, 0.0) * wrows[h]
            if masked:
                kpos = j * QB + keyi
                tot = jnp.where(kpos < PADF, MASK_BIG, tot)
                tot = jnp.where((kpos <= qpos) & (kpos >= DEAD), tot, -MASK_BIG)
            bits = pltpu.bitcast(tot + 0.0, I32)
            key = bits ^ ((bits >> 31) & 0x7FFFFFFF)
            sc_ref[j] = key
            hi_ref[j] = (key >> 16).astype(I16)
            lo_ref[j] = ((key & 0xFFFF) - 32768).astype(I16)

    score_grp(0, True)

    def interior(g, carry):
        score_grp(g, False)
        return carry

    lax.fori_loop(1, ngrp - 1, interior, 0)

    @pl.when(ngrp > 1)
    def _():
        score_grp(ngrp - 1, True)

    one = jnp.ones((QB, QB), I16)
    zero = jnp.zeros((QB, QB), I16)

    def count16(ref, pred):
        def tree(xs):
            while len(xs) > 1:
                xs = [a + b for a, b in zip(xs[0::2], xs[1::2])]
            return xs[0]

        def cb(g, a):
            parts = []
            for u in range(G):
                x = jnp.where(pred(ref[g * G + u]), one, zero)
                parts.append(tree([x[16 * r:16 * r + 16] for r in range(QB // 16)]))
            return a + tree(parts)
        a = lax.fori_loop(0, ngrp, cb, jnp.zeros((16, QB), I16))
        return jnp.sum(a.astype(I32), axis=0, keepdims=True)

    def radix16(ref, want):
        def bit_body(bi, c):
            trial = c + lax.shift_left(jnp.int32(1), 15 - bi)
            t16 = trial.astype(I16)
            cnt = count16(ref, lambda x: x >= t16)
            return jnp.where(cnt >= want, trial, c)
        return lax.fori_loop(0, 16, bit_body, jnp.full((1, QB), -32768, I32))

    c_hi = radix16(hi_ref, n_sel)
    c_hi16 = c_hi.astype(I16)
    want_lo = n_sel - count16(hi_ref, lambda x: x > c_hi16)

    def band_grp(g, carry):
        for u in range(G):
            j = g * G + u
            lo_ref[j] = jnp.where(hi_ref[j] == c_hi16, lo_ref[j], jnp.full((QB, QB), -32768, I16))
        return carry

    lax.fori_loop(0, ngrp, band_grp, 0)
    c_lo = radix16(lo_ref, want_lo)
    c_lo16 = c_lo.astype(I16)
    c = (c_hi << 16) | (c_lo + 32768)
    need = (want_lo - count16(lo_ref, lambda x: x > c_lo16)).astype(F32)

    m_ref[...] = jnp.full(m_ref.shape, -MASK_BIG, F32)
    acc_ref[...] = jnp.zeros(acc_ref.shape, F32)
    tri = tri_ref[...]
    need = jnp.where(c > KEY_NEG, need, -1.0)

    def att_grp(g, eqc):
        for u in range(G):
            key = sc_ref[g * G + u]
            eqf = jnp.where(key == c, 1.0, 0.0)
            pref = jnp.dot(tri, eqf.astype(BF16), preferred_element_type=F32)
            sel = (key > c) | ((key == c) & (pref <= need - eqc))
            eqc = eqc + jnp.sum(eqf, axis=0, keepdims=True)
            bias_ref[u * QB:(u + 1) * QB, :] = jnp.where(sel, 0.0, -MASK_BIG)
        r0 = pl.multiple_of(g * (G * QB), G * QB)
        kg = k_ref[pl.ds(r0, G * QB), :]
        vtg = vt_ref[:, pl.ds(r0, G * QB)]
        for h in range(AT_HEADS):
            s_ref[h] = jnp.dot(kg, qt_ref[h], preferred_element_type=F32) + bias_ref[...]
        alphas = []
        for h in range(AT_HEADS):
            hs = slice(h * QB, (h + 1) * QB)
            m_old = m_ref[:, hs]
            m_new = jnp.maximum(m_old, jnp.max(s_ref[h], axis=0, keepdims=True))
            m_ref[:, hs] = m_new
            p_ref[h] = jnp.exp2(s_ref[h] - m_new).astype(BF16)
            alphas.append(jnp.exp2(m_old - m_new))
        for h in range(AT_HEADS):
            hs = slice(h * QB, (h + 1) * QB)
            acc_ref[:, hs] = alphas[h] * acc_ref[:, hs] + jnp.dot(vtg, p_ref[h], preferred_element_type=F32)
        return eqc

    lax.fori_loop(0, ngrp, att_grp, jnp.zeros((1, QB), F32))

    ot = (acc_ref[0:AT_DH, :] / acc_ref[AT_DH:AT_DH + 1, :]).astype(BF16)
    y = jnp.zeros((QB, D_MODEL), F32)
    for h in range(AT_HEADS):
        y = y + lax.dot_general(ot[:, h * QB:(h + 1) * QB], wo_ref[h], _TN, preferred_element_type=F32)
    o_ref[...] = y.astype(BF16)


def _dsa_main(qt, qit, wt, k, vt, ki, wo, B, Tp, n_sel):
    R = k.shape[0]
    QB = Q_BLOCK
    nb = Tp // QB
    nbx = -(-nb // DSA_GROUP) * DSA_GROUP
    Tx = nbx * QB
    pad_rows = lambda a: jnp.pad(a.reshape(B, Tp, -1), ((0, 0), (0, Tx - Tp), (0, 0))).reshape(B * Tx, -1)
    k, ki = pad_rows(k), pad_rows(ki)
    vt = jnp.pad(vt.reshape(AT_DH, B, Tp), ((0, 0), (0, 0), (0, Tx - Tp))).reshape(AT_DH, B * Tx)
    vt = jnp.concatenate([vt, jnp.ones((1, B * Tx), BF16), jnp.zeros((VT_ROWS - AT_DH - 1, B * Tx), BF16)], axis=0)
    tri = jnp.asarray(np.tril(np.ones((QB, QB), np.float32)), BF16)
    seq = lambda: pl.BlockSpec((Tx, 64), lambda b, i: (b, 0))
    return pl.pallas_call(
        functools.partial(_dsa_body, n_sel=n_sel),
        grid=(B, nb),
        in_specs=[pl.BlockSpec((AT_HEADS, AT_DH, QB), lambda b, i: (0, 0, b * nb + i)),
                  pl.BlockSpec((IDX_HEADS, IDX_DIM, QB), lambda b, i: (0, 0, b * nb + i)),
                  pl.BlockSpec((8, QB), lambda b, i: (0, b * nb + i)),
                  seq(), pl.BlockSpec((VT_ROWS, Tx), lambda b, i: (0, b)), seq(),
                  _const_spec(wo.shape), _const_spec(tri.shape)],
        out_specs=pl.BlockSpec((QB, D_MODEL), lambda b, i: (b * nb + i, 0)),
        out_shape=jax.ShapeDtypeStruct((R, D_MODEL), BF16),
        scratch_shapes=[pltpu.VMEM((nbx, QB, QB), I32), pltpu.VMEM((nbx, QB, QB), jnp.int16),
                        pltpu.VMEM((nbx, QB, QB), jnp.int16), pltpu.VMEM((DSA_GROUP * QB, QB), F32),
                        pltpu.VMEM((AT_HEADS, DSA_GROUP * QB, QB), F32),
                        pltpu.VMEM((AT_HEADS, DSA_GROUP * QB, QB), BF16),
                        pltpu.VMEM((1, AT_HEADS * QB), F32), pltpu.VMEM((VT_ROWS, AT_HEADS * QB), F32)],
        compiler_params=_params(("parallel", "arbitrary")),
        name="dsa_attention",
    )(qt, qit, wt, k, vt, ki, wo, tri)


def _mix_body(ohg_ref, yat_ref, ocv_ref, gt_ref, h_ref, whg_ref, wcv_ref, wmix_ref, o_ref):
    y_hg = jnp.dot(ohg_ref[...], whg_ref[...], preferred_element_type=F32)
    y_cv = jnp.dot(ocv_ref[...], wcv_ref[...], preferred_element_type=F32)
    y_at = yat_ref[...].astype(F32)
    g1 = _sigmoid(gt_ref[:, 0:D_MODEL].astype(F32))
    g2 = _sigmoid(gt_ref[:, D_MODEL:2 * D_MODEL].astype(F32))
    g3 = _sigmoid(gt_ref[:, 2 * D_MODEL:3 * D_MODEL].astype(F32))
    mixed = (g1 * y_hg + g2 * y_at + g3 * y_cv).astype(BF16)
    o_ref[...] = h_ref[...] + jnp.dot(mixed, wmix_ref[...], preferred_element_type=F32)


def _mix(ohg, yat, ocv, gates, h, whg, wcv, wmix):
    R, D = h.shape
    tm = _pick(R, (512, 256, 128))
    rows = lambda n: pl.BlockSpec((tm, n), lambda i: (i, 0))
    return pl.pallas_call(
        _mix_body,
        grid=(R // tm,),
        in_specs=[rows(ohg.shape[1]), rows(D), rows(ocv.shape[1]), rows(3 * D), rows(D),
                  _const_spec(whg.shape), _const_spec(wcv.shape), _const_spec(wmix.shape)],
        out_specs=rows(D),
        out_shape=jax.ShapeDtypeStruct((R, D), F32),
        compiler_params=_params(("parallel",)),
        name="branch_mix",
    )(ohg, yat, ocv, gates, h, whg, wcv, wmix)


def _ffn_body(h_ref, hh_ref, g_ref, wa_ref, wb_ref, cwa_ref, cwb_ref, cba_ref, cbb_ref, wd_ref, o_ref,
              xn_scr, ua_scr, ub_scr, acc_scr, *, tm):
    it = pl.program_id(1)
    g = g_ref[...]
    xn_scr[0:FF_HALO] = _rms(hh_ref[...], g).astype(BF16)
    xn_scr[FF_HALO:FF_HALO + tm] = _rms(h_ref[...], g).astype(BF16)
    pos = it * tm - FF_HALO + lax.broadcasted_iota(I32, (tm + FF_HALO, 1), 0)
    live = pos >= DEAD
    acc_scr[...] = jnp.zeros(acc_scr.shape, F32)

    def conv(scr, cw, cb):
        y = cb
        for j in range(FF_CONV):
            y = y + cw[j:j + 1] * scr[FF_HALO - (FF_CONV - 1) + j:FF_HALO - (FF_CONV - 1) + j + tm]
        return y

    def chunk(c, carry):
        xn = xn_scr[...]
        ua_scr[...] = jnp.where(live, jnp.dot(xn, wa_ref[c], preferred_element_type=F32), 0.0)
        ub_scr[...] = jnp.where(live, jnp.dot(xn, wb_ref[c], preferred_element_type=F32), 0.0)
        a = conv(ua_scr, cwa_ref[c], cba_ref[c])
        b = conv(ub_scr, cwb_ref[c], cbb_ref[c])
        act = (a * _sigmoid(a) * b).astype(BF16)
        acc_scr[...] += jnp.dot(act, wd_ref[c], preferred_element_type=F32)
        return carry

    lax.fori_loop(0, N_FF_CHUNK, chunk, 0)
    o_ref[...] = h_ref[...] + acc_scr[...]


def _ffn(h, g, wa, wb, cwa, cwb, cba, cbb, wd, B, Tp):
    R, D = h.shape
    tm = _pick(Tp, (640, 384, 128))
    nt = Tp // tm
    hb = tm // FF_HALO
    return pl.pallas_call(
        functools.partial(_ffn_body, tm=tm),
        grid=(B, nt),
        in_specs=[pl.BlockSpec((tm, D), lambda b, t: (b * nt + t, 0)),
                  pl.BlockSpec((FF_HALO, D), lambda b, t: (jnp.maximum((b * nt + t) * hb - 1, 0), 0)),
                  _const_spec((1, D))] + [_const_spec(w.shape) for w in (wa, wb, cwa, cwb, cba, cbb, wd)],
        out_specs=pl.BlockSpec((tm, D), lambda b, t: (b * nt + t, 0)),
        out_shape=jax.ShapeDtypeStruct((R, D), F32),
        scratch_shapes=[pltpu.VMEM((FF_HALO + tm, D), BF16),
                        pltpu.VMEM((FF_HALO + tm, FF_CHUNK), F32), pltpu.VMEM((FF_HALO + tm, FF_CHUNK), F32),
                        pltpu.VMEM((tm, D), F32)],
        compiler_params=_params(("parallel", "parallel")),
        name="conv_ffn",
    )(h, h, g, wa, wb, cwa, cwb, cba, cbb, wd)


def _pack_in_proj(w):
    HW = HG_HEADS * HG_DK
    o = 4 * HW
    cq = w[:, o:o + AT_QRANK]
    o += AT_QRANK
    pieces = [cq]
    for n in (AT_DH, AT_DH, IDX_DIM, IDX_HEADS):
        pieces.append(jnp.pad(w[:, o:o + n], ((0, 0), (0, 128 - n))))
        o += n
    w_at = jnp.concatenate(pieces, axis=1)
    w_cv = w[:, o:o + 2 * CV_CH]
    o += 2 * CV_CH
    w_gt = w[:, o:o + N_BRANCH * D_MODEL]
    return [a.astype(BF16) for a in (w[:, :4 * HW], w_at, w_cv, w_gt)]


def _chunk_cols(w):
    return w.reshape(w.shape[0], N_FF_CHUNK, FF_CHUNK).transpose(1, 0, 2)


def kernel(x, meta_tokens, hgrn_lb, norm1_g, w_in, hg_norm_g, w_hg_out, cq_norm_g, w_uq, w_qi, q_norm_g, k_norm_g, w_at_out, cv_dw_w, cv_dw_b, cv_ln_g, cv_ln_b, w_cv_out, w_mix_out, norm2_g, w_ffn_up, ffn_dw_w, ffn_dw_b, w_ffn_down):
    B, SEQ, D = x.shape
    depth = w_in.shape[0]
    Tp = PADF + SEQ
    R = B * Tp
    n_sel = min(TOPK_MAX, SEQ // 4)

    meta = jnp.broadcast_to(meta_tokens.astype(x.dtype)[None], (B, N_META, D))
    h = jnp.concatenate([jnp.zeros((B, DEAD, D), x.dtype), meta, x], axis=1).reshape(R, D)
    p = jax.nn.softmax(hgrn_lb.astype(F32), axis=0)
    lbs = jnp.cumsum(p, axis=0) - p[0]

    row = lambda a: a.reshape(1, -1).astype(F32)
    for l in range(depth):
        w_hg, w_at, w_cv, w_gt = _pack_in_proj(w_in[l])
        zh, zatt, zu, zg = _in_proj(h, row(norm1_g[l]), w_hg, w_at, w_cv, w_gt)

        o_hg = _hgrn(zh, row(lbs[l]), row(hg_norm_g[l]), B, Tp)
        o_cv = _conv_branch(zu, cv_dw_w[l].astype(F32), row(cv_dw_b[l]), row(cv_ln_g[l]), row(cv_ln_b[l]), B, Tp)

        wuq_t = w_uq[l].reshape(AT_QRANK, AT_HEADS, AT_DH).transpose(1, 2, 0).astype(BF16)
        wqi_t = w_qi[l].reshape(AT_QRANK, IDX_HEADS, IDX_DIM).transpose(1, 2, 0).astype(BF16)
        qt, qit, k, vt, ki, wts = _dsa_prep(zatt, row(cq_norm_g[l]), wuq_t, wqi_t,
                                            q_norm_g[l].reshape(-1, 1).astype(F32), row(k_norm_g[l]))
        wt = jnp.pad(wts[:, :IDX_HEADS].T, ((0, 8 - IDX_HEADS), (0, 0)))
        wo = w_at_out[l].reshape(AT_HEADS, AT_DH, D).astype(BF16)
        y_at = _dsa_main(qt, qit, wt, k, vt, ki, wo, B, Tp, n_sel)

        h = _mix(o_hg, y_at, o_cv, zg, h, w_hg_out[l].astype(BF16), w_cv_out[l].astype(BF16),
                 w_mix_out[l].astype(BF16))

        wa = _chunk_cols(w_ffn_up[l][:, :FF_DIM]).astype(BF16)
        wb = _chunk_cols(w_ffn_up[l][:, FF_DIM:]).astype(BF16)
        cwa = _chunk_cols(ffn_dw_w[l][:, :FF_DIM]).astype(F32)
        cwb = _chunk_cols(ffn_dw_w[l][:, FF_DIM:]).astype(F32)
        cba = _chunk_cols(ffn_dw_b[l][None, :FF_DIM]).astype(F32)
        cbb = _chunk_cols(ffn_dw_b[l][None, FF_DIM:]).astype(F32)
        wd = w_ffn_down[l].reshape(N_FF_CHUNK, FF_CHUNK, D).astype(BF16)
        h = _ffn(h, row(norm2_g[l]), wa, wb, cwa, cwb, cba, cbb, wd, B, Tp)

    return h.reshape(B, Tp, D)[:, PADF:]
```

```python
import functools

import numpy as np
import jax
import jax.numpy as jnp
from jax import lax
from jax.experimental import pallas as pl
from jax.experimental.pallas import tpu as pltpu

F32 = jnp.float32
BF16 = jnp.bfloat16
I32 = jnp.int32

D_MODEL = 1024
N_META = 16
HG_HEADS = 4
HG_DK = 128
HG_DV = 128
HG_CHUNK = 64
AT_HEADS = 8
AT_DH = 64
AT_QRANK = 256
IDX_HEADS = 4
IDX_DIM = 64
TOPK_MAX = 256
Q_BLOCK = 128
CV_CH = 512
CV_WIDTH = 31
FF_DIM = 2816
FF_CONV = 3
N_BRANCH = 3
EPS = 1e-6
MASK_BIG = 1e30

PADF = Q_BLOCK
DEAD = PADF - N_META
FF_CHUNK = 256
N_FF_CHUNK = FF_DIM // FF_CHUNK
ATT_SLAB = 768
CV_HALO = 32
FF_HALO = 16
DSA_GROUP = 4
LOG2E = 1.4426950408889634
VT_ROWS = AT_DH + 16
KEY_NEG = int(np.array(-MASK_BIG, np.float32).view(np.int32)) ^ 0x7FFFFFFF
VMEM_LIMIT = 56 * 1024 * 1024

_NT = (((1,), (1,)), ((), ()))
_TN = (((0,), (0,)), ((), ()))


def _pick(n, cands):
    for c in cands:
        if n % c == 0:
            return c
    raise ValueError(f"no tile for {n}")


def _const_spec(shape):
    nd = len(shape)
    return pl.BlockSpec(shape, lambda *_: (0,) * nd, pipeline_mode=pl.Buffered(1))


def _params(sem):
    return pltpu.CompilerParams(dimension_semantics=sem, vmem_limit_bytes=VMEM_LIMIT)


def _sigmoid(x):
    return 1.0 / (1.0 + jnp.exp(-x))


def _rms(x, g):
    return x * lax.rsqrt(jnp.mean(x * x, axis=-1, keepdims=True) + EPS) * g


def _in_proj_body(h_ref, g_ref, w_hg, w_at, w_cv, w_gt, o_hg, o_at, o_cv, o_gt):
    xn = _rms(h_ref[...], g_ref[...]).astype(BF16)
    for w, o in ((w_hg, o_hg), (w_at, o_at), (w_cv, o_cv), (w_gt, o_gt)):
        o[...] = jnp.dot(xn, w[...], preferred_element_type=F32).astype(o.dtype)


def _in_proj(h, g, w_hg, w_at, w_cv, w_gt):
    R, D = h.shape
    tm = _pick(R, (256, 128))
    outs = ((w_hg.shape[1], F32), (w_at.shape[1], F32), (w_cv.shape[1], BF16), (w_gt.shape[1], BF16))
    return pl.pallas_call(
        _in_proj_body,
        grid=(R // tm,),
        in_specs=[pl.BlockSpec((tm, D), lambda i: (i, 0)), _const_spec((1, D))]
        + [_const_spec(w.shape) for w in (w_hg, w_at, w_cv, w_gt)],
        out_specs=[pl.BlockSpec((tm, n), lambda i: (i, 0)) for n, _ in outs],
        out_shape=[jax.ShapeDtypeStruct((R, n), dt) for n, dt in outs],
        compiler_params=_params(("parallel",)),
        name="in_proj",
    )(h, g, w_hg, w_at, w_cv, w_gt)


def _hgrn_consts():
    C = HG_CHUNK
    t = np.arange(C)[:, None]
    s = np.arange(C)[None, :]
    mats = [(s <= t)]
    for m in (8, 16, 32):
        rho = (t // (2 * m)) * (2 * m) + m - 1
        mats.append(s <= rho)
    return np.concatenate(mats, axis=0).astype(np.float32)


def _split3(x):
    hi = x.astype(BF16)
    r1 = x - hi.astype(F32)
    mid = r1.astype(BF16)
    lo = (r1 - mid.astype(F32)).astype(BF16)
    return hi, mid, lo


def _hgrn_body(q_ref, f_ref, i_ref, g_ref, lb_ref, ng_ref, cm_ref, o_ref, st_ref, *, tc):
    C = HG_CHUNK
    it = pl.program_id(1)

    @pl.when(it == 0)
    def _():
        st_ref[...] = jnp.zeros_like(st_ref)

    row = lax.broadcasted_iota(I32, (C, C), 0)
    col = lax.broadcasted_iota(I32, (C, C), 1)
    m_diag = (row // 8 == col // 8) & (col <= row)
    level_masks = []
    for m in (8, 16, 32):
        level_masks.append((row // (2 * m) == col // (2 * m)) & ((row // m) % 2 == 1) & ((col // m) % 2 == 0))
    rid = lax.broadcasted_iota(I32, (C, 1), 0)
    second = [(rid // m) % 2 == 1 for m in (8, 16, 32)]
    lane_c = lax.broadcasted_iota(I32, (8, C), 1)
    cm = cm_ref[...]
    ng = ng_ref[...]

    def chunk(c, carry):
        r0 = pl.multiple_of(c * C, C)
        live = (it * tc + r0 + rid) >= DEAD
        for h in range(HG_HEADS):
            cs = slice(h * HG_DK, (h + 1) * HG_DK)
            lb = lb_ref[:, cs]
            f = lb + (1.0 - lb) * _sigmoid(f_ref[pl.ds(r0, C), cs])
            lf = jnp.where(live, jnp.log(f), 0.0)
            k = jnp.where(live, 1.0 - f, 0.0)
            qr = q_ref[pl.ds(r0, C), cs]
            q = qr * _sigmoid(qr) * (HG_DK ** -0.5)
            v = i_ref[pl.ds(r0, C), cs].astype(BF16)
            cums = sum(jnp.dot(cm, part, preferred_element_type=F32) for part in _split3(lf))
            b = cums[0:C]
            att = jnp.zeros((C, C), F32)
            for lvl in range(3):
                r = cums[(lvl + 1) * C:(lvl + 2) * C]
                e = jnp.exp(jnp.minimum(jnp.where(second[lvl], b - r, r - b), 0.0))
                a = lax.dot_general((q * e).astype(BF16), (k * e).astype(BF16), _NT, preferred_element_type=F32)
                att = jnp.where(level_masks[lvl], a, att)
            blocks = []
            for j in range(C // 8):
                bR = b[8 * j:8 * j + 8]
                qR = q[8 * j:8 * j + 8]
                acc = jnp.zeros((8, C), F32)
                for s in range(8 * j, 8 * j + 8):
                    e = jnp.exp(jnp.minimum(bR - b[s:s + 1], 0.0))
                    colv = jnp.sum(qR * (k[s:s + 1] * e), axis=-1, keepdims=True)
                    acc = jnp.where(lane_c == s, colv, acc)
                blocks.append(acc)
            att = jnp.where(m_diag, jnp.concatenate(blocks, axis=0), att)
            st = st_ref[h]
            o = jnp.dot(att.astype(BF16), v, preferred_element_type=F32)
            o = o + lax.dot_general((q * jnp.exp(b)).astype(BF16), st.astype(BF16), _NT, preferred_element_type=F32)
            bl = b[C - 1:C]
            kb = (k * jnp.exp(bl - b)).astype(BF16)
            st_ref[h] = st * jnp.exp(bl) + lax.dot_general(v, kb, _TN, preferred_element_type=F32)
            gr = g_ref[pl.ds(r0, C), cs]
            o_ref[pl.ds(r0, C), cs] = (_rms(o, ng) * (gr * _sigmoid(gr))).astype(BF16)
        return carry

    lax.fori_loop(0, tc // C, chunk, 0)


def _hgrn(zh, lb, ng, B, Tp):
    R = zh.shape[0]
    W = HG_HEADS * HG_DK
    tc = _pick(Tp, (640, 384, 128))
    nt = Tp // tc
    cm = jnp.asarray(_hgrn_consts(), BF16)

    def col(cb):
        return pl.BlockSpec((tc, W), lambda b, t: (b * nt + t, cb))

    return pl.pallas_call(
        functools.partial(_hgrn_body, tc=tc),
        grid=(B, nt),
        in_specs=[col(0), col(1), col(2), col(3), _const_spec((1, W)), _const_spec((1, HG_DV)),
                  _const_spec(cm.shape)],
        out_specs=pl.BlockSpec((tc, W), lambda b, t: (b * nt + t, 0)),
        out_shape=jax.ShapeDtypeStruct((R, W), BF16),
        scratch_shapes=[pltpu.VMEM((HG_HEADS, HG_DV, HG_DK), F32)],
        compiler_params=_params(("parallel", "arbitrary")),
        name="hgrn2",
    )(zh, zh, zh, zh, lb, ng, cm)


def _conv_body(u_ref, uh_ref, dw_ref, db_ref, lg_ref, lb_ref, o_ref, scr, sh_scr, *, tt):
    it = pl.program_id(1)
    RC = 32
    L = tt + CV_HALO - 8

    def glu(u, pos0):
        u = u.astype(F32)
        hh = u[:, :CV_CH] * _sigmoid(u[:, CV_CH:])
        pos = pos0 + lax.broadcasted_iota(I32, (u.shape[0], 1), 0)
        return jnp.where(pos >= DEAD, hh, 0.0)

    scr[0:CV_HALO] = glu(uh_ref[...], it * tt - CV_HALO)
    scr[CV_HALO:CV_HALO + tt] = glu(u_ref[...], it * tt)
    for s in range(1, 8):
        sh_scr[s - 1] = scr[s:s + L]
    bias = db_ref[...]
    lg = lg_ref[...]
    lbias = lb_ref[...]

    def chunk(c, carry):
        r0 = pl.multiple_of(c * RC, RC)
        acc = jnp.zeros((RC, CV_CH), F32) + bias
        for j in range(CV_WIDTH):
            off = CV_HALO - CV_WIDTH + 1 + j
            base = pl.ds(r0 + (off // 8) * 8, RC)
            xs = scr[base, :] if off % 8 == 0 else sh_scr[off % 8 - 1, base, :]
            acc = acc + dw_ref[j:j + 1, :] * xs
        mu = jnp.mean(acc, axis=-1, keepdims=True)
        xc = acc - mu
        y = xc * lax.rsqrt(jnp.mean(xc * xc, axis=-1, keepdims=True) + EPS) * lg + lbias
        o_ref[pl.ds(r0, RC), :] = (y * _sigmoid(y)).astype(BF16)
        return carry

    lax.fori_loop(0, tt // RC, chunk, 0)


def _conv_branch(zu, dw, db, lg, lb, B, Tp):
    R = zu.shape[0]
    tt = _pick(Tp, (640, 384, 128))
    nt = Tp // tt
    hb = tt // CV_HALO
    return pl.pallas_call(
        functools.partial(_conv_body, tt=tt),
        grid=(B, nt),
        in_specs=[pl.BlockSpec((tt, 2 * CV_CH), lambda b, t: (b * nt + t, 0)),
                  pl.BlockSpec((CV_HALO, 2 * CV_CH), lambda b, t: (jnp.maximum((b * nt + t) * hb - 1, 0), 0)),
                  _const_spec(dw.shape), _const_spec((1, CV_CH)), _const_spec((1, CV_CH)), _const_spec((1, CV_CH))],
        out_specs=pl.BlockSpec((tt, CV_CH), lambda b, t: (b * nt + t, 0)),
        out_shape=jax.ShapeDtypeStruct((R, CV_CH), BF16),
        scratch_shapes=[pltpu.VMEM((CV_HALO + tt, CV_CH), F32), pltpu.VMEM((7, CV_HALO + tt - 8, CV_CH), F32)],
        compiler_params=_params(("parallel", "parallel")),
        name="conformer_conv",
    )(zu, zu, dw, db, lg, lb)


def _dsa_prep_body(z_ref, cg_ref, wuq_ref, wqi_ref, qg_ref, kg_ref, eye_ref,
                   q_ref, qi_ref, k_ref, vt_ref, ki_ref, w_ref):
    z = z_ref[...]
    cqn = _rms(z[:, 0:AT_QRANK], cg_ref[...]).astype(BF16)
    qg = qg_ref[...]
    for h in range(AT_HEADS):
        qh = lax.dot_general(wuq_ref[h], cqn, _NT, preferred_element_type=F32)
        qh = qh * lax.rsqrt(jnp.mean(qh * qh, axis=0, keepdims=True) + EPS) * qg
        q_ref[h] = (qh * (AT_DH ** -0.5 * LOG2E)).astype(BF16)
    for h in range(IDX_HEADS):
        qi_ref[h] = lax.dot_general(wqi_ref[h], cqn, _NT, preferred_element_type=F32).astype(BF16)
    k_ref[...] = _rms(z[:, 256:256 + AT_DH], kg_ref[...]).astype(BF16)
    vt_ref[...] = lax.dot_general(eye_ref[...], z[:, 384:384 + AT_DH].astype(BF16), _NT,
                                  preferred_element_type=F32).astype(BF16)
    ki_ref[...] = z[:, 512:512 + IDX_DIM].astype(BF16)
    w_ref[...] = z[:, 640:768] * ((IDX_HEADS * IDX_DIM) ** -0.5)


def _dsa_prep(zatt, cg, wuq_t, wqi_t, qg_col, kg):
    R = zatt.shape[0]
    tm = _pick(R, (512, 256, 128))
    row64 = pl.BlockSpec((tm, 64), lambda i: (i, 0))
    eye = jnp.eye(AT_DH, dtype=BF16)
    return pl.pallas_call(
        _dsa_prep_body,
        grid=(R // tm,),
        in_specs=[pl.BlockSpec((tm, ATT_SLAB), lambda i: (i, 0)), _const_spec((1, AT_QRANK)),
                  _const_spec(wuq_t.shape), _const_spec(wqi_t.shape), _const_spec((AT_DH, 1)),
                  _const_spec((1, AT_DH)), _const_spec(eye.shape)],
        out_specs=[pl.BlockSpec((AT_HEADS, AT_DH, tm), lambda i: (0, 0, i)),
                   pl.BlockSpec((IDX_HEADS, IDX_DIM, tm), lambda i: (0, 0, i)),
                   row64, pl.BlockSpec((AT_DH, tm), lambda i: (0, i)), row64,
                   pl.BlockSpec((tm, 128), lambda i: (i, 0))],
        out_shape=[jax.ShapeDtypeStruct((AT_HEADS, AT_DH, R), BF16),
                   jax.ShapeDtypeStruct((IDX_HEADS, IDX_DIM, R), BF16),
                   jax.ShapeDtypeStruct((R, AT_DH), BF16), jax.ShapeDtypeStruct((AT_DH, R), BF16),
                   jax.ShapeDtypeStruct((R, IDX_DIM), BF16), jax.ShapeDtypeStruct((R, 128), F32)],
        compiler_params=_params(("parallel",)),
        name="dsa_prep",
    )(zatt, cg, wuq_t, wqi_t, qg_col, kg, eye)


def _dsa_body(qt_ref, qit_ref, wt_ref, k_ref, vt_ref, ki_ref, wo_ref, tri_ref, o_ref,
              sc_ref, hi_ref, lo_ref, kaug_ref, qaug_ref, s_ref, p_ref, m_ref, acc_ref, *, n_sel):
    QB = Q_BLOCK
    G = DSA_GROUP
    I16 = jnp.int16
    qb = pl.program_id(1)
    ngrp = (qb + G) // G
    keyi = lax.broadcasted_iota(I32, (QB, QB), 0)
    qpos = qb * QB + lax.broadcasted_iota(I32, (QB, QB), 1)

    qits = [qit_ref[h] for h in range(IDX_HEADS)]
    wrows = [wt_ref[h:h + 1, :] for h in range(IDX_HEADS)]

    def score_grp(g, masked):
        for u in range(G):
            j = g * G + u
            kij = ki_ref[pl.ds(pl.multiple_of(j * QB, QB), QB), :]
            tot = jnp.zeros((QB, QB), F32)
            for h in range(IDX_HEADS):
                s = jnp.dot(kij, qits[h], preferred_element_type=F32)
                tot = tot + jnp.maximum(s, 0.0) * wrows[h]
            if masked:
                kpos = j * QB + keyi
                tot = jnp.where(kpos < PADF, MASK_BIG, tot)
                tot = jnp.where((kpos <= qpos) & (kpos >= DEAD), tot, -MASK_BIG)
            bits = pltpu.bitcast(tot + 0.0, I32)
            key = bits ^ ((bits >> 31) & 0x7FFFFFFF)
            sc_ref[j] = key
            hi_ref[j] = (key >> 16).astype(I16)
            lo_ref[j] = ((key & 0xFFFF) - 32768).astype(I16)

    score_grp(0, True)

    def interior(g, carry):
        score_grp(g, False)
        return carry

    lax.fori_loop(1, ngrp - 1, interior, 0)

    @pl.when(ngrp > 1)
    def _():
        score_grp(ngrp - 1, True)

    @pl.when(ngrp % 2 == 1)
    def _():
        for u in range(G):
            sc_ref[ngrp * G + u] = jnp.full((QB, QB), KEY_NEG, I32)
            hi_ref[ngrp * G + u] = jnp.full((QB, QB), -32768, I16)
            lo_ref[ngrp * G + u] = jnp.full((QB, QB), -32768, I16)

    one = jnp.ones((QB, QB), I16)
    zero = jnp.zeros((QB, QB), I16)

    def count16(ref, pred):
        def tree(xs):
            while len(xs) > 1:
                xs = [a + b for a, b in zip(xs[0::2], xs[1::2])]
            return xs[0]

        def cb(g, a):
            parts = []
            for u in range(2 * G):
                x = jnp.where(pred(ref[g * (2 * G) + u]), one, zero)
                parts.append(tree([x[16 * r:16 * r + 16] for r in range(QB // 16)]))
            return a + tree(parts)
        a = lax.fori_loop(0, (ngrp + 1) // 2, cb, jnp.zeros((16, QB), I16))
        return jnp.sum(a.astype(I32), axis=0, keepdims=True)

    def radix16(ref, want):
        def bit_body(bi, c):
            trial = c + lax.shift_left(jnp.int32(1), 15 - bi)
            t16 = trial.astype(I16)
            cnt = count16(ref, lambda x: x >= t16)
            return jnp.where(cnt >= want, trial, c)
        return lax.fori_loop(0, 16, bit_body, jnp.full((1, QB), -32768, I32))

    c_hi = radix16(hi_ref, n_sel)
    c_hi16 = c_hi.astype(I16)
    want_lo = n_sel - count16(hi_ref, lambda x: x > c_hi16)

    def band_grp(g, carry):
        for u in range(G):
            j = g * G + u
            lo_ref[j] = jnp.where(hi_ref[j] == c_hi16, lo_ref[j], jnp.full((QB, QB), -32768, I16))
        return carry

    lax.fori_loop(0, ngrp, band_grp, 0)
    c_lo = radix16(lo_ref, want_lo)
    c_lo16 = c_lo.astype(I16)
    c = (c_hi << 16) | (c_lo + 32768)
    need = (want_lo - count16(lo_ref, lambda x: x > c_lo16)).astype(F32)

    m_ref[...] = jnp.full(m_ref.shape, -3.0e38, F32)
    acc_ref[...] = jnp.zeros(acc_ref.shape, F32)
    tri = tri_ref[...]
    need = jnp.where(c > KEY_NEG, need, -1.0)

    rr = lax.broadcasted_iota(I32, (QB, 2 * QB), 0)
    cc = lax.broadcasted_iota(I32, (QB, 2 * QB), 1)
    eye2 = jnp.where((cc == rr) | (cc == rr + QB), 1.0, 0.0).astype(BF16)
    for hp in range(AT_HEADS // 2):
        qaug_ref[hp, 0:QB, :] = eye2
        qaug_ref[hp, QB:QB + AT_DH, 0:QB] = qt_ref[2 * hp]
        qaug_ref[hp, QB:QB + AT_DH, QB:2 * QB] = qt_ref[2 * hp + 1]

    def scores(g, eqc, slot):
        for u in range(G):
            key = sc_ref[g * G + u]
            eqf = jnp.where(key == c, 1.0, 0.0)
            pref = jnp.dot(tri, eqf.astype(BF16), preferred_element_type=F32)
            sel = (key > c) | ((key == c) & (pref <= need - eqc))
            eqc = eqc + jnp.sum(eqf, axis=0, keepdims=True)
            kaug_ref[slot, u * QB:(u + 1) * QB, 0:QB] = jnp.where(sel, 0.0, -MASK_BIG).astype(BF16)
        r0 = pl.multiple_of(g * (G * QB), G * QB)
        kaug_ref[slot, :, QB:QB + AT_DH] = k_ref[pl.ds(r0, G * QB), :]
        kaug = kaug_ref[slot]
        for hp in range(AT_HEADS // 2):
            s_ref[slot, hp] = jnp.dot(kaug, qaug_ref[hp], preferred_element_type=F32)
        return eqc

    def attend(g, slot):
        r0 = pl.multiple_of(g * (G * QB), G * QB)
        vtg = vt_ref[:, pl.ds(r0, G * QB)]
        alphas = []
        for hp in range(AT_HEADS // 2):
            hs = slice(hp * 2 * QB, (hp + 1) * 2 * QB)
            m_old = m_ref[:, hs]
            m_new = jnp.maximum(m_old, jnp.max(s_ref[slot, hp], axis=0, keepdims=True))
            m_ref[:, hs] = m_new
            p_ref[slot, hp] = jnp.exp2(s_ref[slot, hp] - m_new).astype(BF16)
            alphas.append(jnp.exp2(m_old - m_new))
        for hp in range(AT_HEADS // 2):
            hs = slice(hp * 2 * QB, (hp + 1) * 2 * QB)
            acc_ref[:, hs] = alphas[hp] * acc_ref[:, hs] + jnp.dot(vtg, p_ref[slot, hp],
                                                                   preferred_element_type=F32)

    npair = (ngrp + 1) // 2
    last = 2 * npair - 1

    def att_pair(i, eqc):
        eqc = scores(2 * i + 1, eqc, 1)
        attend(2 * i, 0)
        eqc_next = scores(jnp.minimum(2 * i + 2, last), eqc, 0)
        attend(2 * i + 1, 1)
        return eqc_next

    lax.fori_loop(0, npair, att_pair, scores(0, jnp.zeros((1, QB), F32), 0))

    ot = (acc_ref[0:AT_DH, :] / acc_ref[AT_DH:AT_DH + 1, :]).astype(BF16)
    y = jnp.zeros((QB, D_MODEL), F32)
    for h in range(AT_HEADS):
        y = y + lax.dot_general(ot[:, h * QB:(h + 1) * QB], wo_ref[h], _TN, preferred_element_type=F32)
    o_ref[...] = y.astype(BF16)


def _dsa_main(qt, qit, wt, k, vt, ki, wo, B, Tp, n_sel):
    R = k.shape[0]
    QB = Q_BLOCK
    nb = Tp // QB
    nbc = -(-nb // (2 * DSA_GROUP)) * (2 * DSA_GROUP)
    assert n_sel <= DSA_GROUP * QB
    Tx = nbc * QB
    pad_rows = lambda a: jnp.pad(a.reshape(B, Tp, -1), ((0, 0), (0, Tx - Tp), (0, 0))).reshape(B * Tx, -1)
    k, ki = pad_rows(k), pad_rows(ki)
    vt = jnp.pad(vt.reshape(AT_DH, B, Tp), ((0, 0), (0, 0), (0, Tx - Tp))).reshape(AT_DH, B * Tx)
    vt = jnp.concatenate([vt, jnp.ones((1, B * Tx), BF16), jnp.zeros((VT_ROWS - AT_DH - 1, B * Tx), BF16)], axis=0)
    tri = jnp.asarray(np.tril(np.ones((QB, QB), np.float32)), BF16)
    seq = lambda: pl.BlockSpec((Tx, 64), lambda b, i: (b, 0))
    return pl.pallas_call(
        functools.partial(_dsa_body, n_sel=n_sel),
        grid=(B, nb),
        in_specs=[pl.BlockSpec((AT_HEADS, AT_DH, QB), lambda b, i: (0, 0, b * nb + i)),
                  pl.BlockSpec((IDX_HEADS, IDX_DIM, QB), lambda b, i: (0, 0, b * nb + i)),
                  pl.BlockSpec((8, QB), lambda b, i: (0, b * nb + i)),
                  seq(), pl.BlockSpec((VT_ROWS, Tx), lambda b, i: (0, b)), seq(),
                  _const_spec(wo.shape), _const_spec(tri.shape)],
        out_specs=pl.BlockSpec((QB, D_MODEL), lambda b, i: (b * nb + i, 0)),
        out_shape=jax.ShapeDtypeStruct((R, D_MODEL), BF16),
        scratch_shapes=[pltpu.VMEM((nbc, QB, QB), I32), pltpu.VMEM((nbc, QB, QB), jnp.int16),
                        pltpu.VMEM((nbc, QB, QB), jnp.int16),
                        pltpu.VMEM((2, DSA_GROUP * QB, QB + AT_DH), BF16),
                        pltpu.VMEM((AT_HEADS // 2, QB + AT_DH, 2 * QB), BF16),
                        pltpu.VMEM((2, AT_HEADS // 2, DSA_GROUP * QB, 2 * QB), F32),
                        pltpu.VMEM((2, AT_HEADS // 2, DSA_GROUP * QB, 2 * QB), BF16),
                        pltpu.VMEM((1, AT_HEADS * QB), F32), pltpu.VMEM((VT_ROWS, AT_HEADS * QB), F32)],
        compiler_params=_params(("parallel", "arbitrary")),
        name="dsa_attention",
    )(qt, qit, wt, k, vt, ki, wo, tri)


def _mix_body(ohg_ref, yat_ref, ocv_ref, gt_ref, h_ref, whg_ref, wcv_ref, wmix_ref, o_ref):
    y_hg = jnp.dot(ohg_ref[...], whg_ref[...], preferred_element_type=F32)
    y_cv = jnp.dot(ocv_ref[...], wcv_ref[...], preferred_element_type=F32)
    y_at = yat_ref[...].astype(F32)
    g1 = _sigmoid(gt_ref[:, 0:D_MODEL].astype(F32))
    g2 = _sigmoid(gt_ref[:, D_MODEL:2 * D_MODEL].astype(F32))
    g3 = _sigmoid(gt_ref[:, 2 * D_MODEL:3 * D_MODEL].astype(F32))
    mixed = (g1 * y_hg + g2 * y_at + g3 * y_cv).astype(BF16)
    o_ref[...] = h_ref[...] + jnp.dot(mixed, wmix_ref[...], preferred_element_type=F32)


def _mix(ohg, yat, ocv, gates, h, whg, wcv, wmix):
    R, D = h.shape
    tm = _pick(R, (512, 256, 128))
    rows = lambda n: pl.BlockSpec((tm, n), lambda i: (i, 0))
    return pl.pallas_call(
        _mix_body,
        grid=(R // tm,),
        in_specs=[rows(ohg.shape[1]), rows(D), rows(ocv.shape[1]), rows(3 * D), rows(D),
                  _const_spec(whg.shape), _const_spec(wcv.shape), _const_spec(wmix.shape)],
        out_specs=rows(D),
        out_shape=jax.ShapeDtypeStruct((R, D), F32),
        compiler_params=_params(("parallel",)),
        name="branch_mix",
    )(ohg, yat, ocv, gates, h, whg, wcv, wmix)


def _ffn_body(h_ref, hh_ref, g_ref, wu_ref, dw_ref, db_ref, wd_ref, o_ref,
              xn_scr, ua_scr, ub_scr, acc_scr, *, tm):
    it = pl.program_id(1)
    g = g_ref[...]
    xn_scr[0:FF_HALO] = _rms(hh_ref[...], g).astype(BF16)
    xn_scr[FF_HALO:FF_HALO + tm] = _rms(h_ref[...], g).astype(BF16)
    pos = it * tm - FF_HALO + lax.broadcasted_iota(I32, (tm + FF_HALO, 1), 0)
    live = pos >= DEAD
    acc_scr[...] = jnp.zeros(acc_scr.shape, F32)

    def conv(scr, slot, col0):
        cols = slice(col0, col0 + FF_CHUNK)
        y = db_ref[:, cols]
        for j in range(FF_CONV):
            r = FF_HALO - (FF_CONV - 1) + j
            y = y + dw_ref[j:j + 1, cols] * scr[slot, r:r + tm]
        return y

    xn = xn_scr[...]
    for c in range(N_FF_CHUNK):
        slot = c % 2
        ca, cb = c * FF_CHUNK, FF_DIM + c * FF_CHUNK
        ua_scr[slot] = jnp.where(live, jnp.dot(xn, wu_ref[:, ca:ca + FF_CHUNK], preferred_element_type=F32), 0.0)
        ub_scr[slot] = jnp.where(live, jnp.dot(xn, wu_ref[:, cb:cb + FF_CHUNK], preferred_element_type=F32), 0.0)
        a = conv(ua_scr, slot, ca)
        b = conv(ub_scr, slot, cb)
        act = (a * _sigmoid(a) * b).astype(BF16)
        acc_scr[...] += jnp.dot(act, wd_ref[ca:ca + FF_CHUNK, :], preferred_element_type=F32)
    o_ref[...] = h_ref[...] + acc_scr[...]


def _ffn(h, g, wu, dw, db, wd, B, Tp):
    R, D = h.shape
    tm = _pick(Tp, (640, 384, 128))
    nt = Tp // tm
    hb = tm // FF_HALO
    return pl.pallas_call(
        functools.partial(_ffn_body, tm=tm),
        grid=(B, nt),
        in_specs=[pl.BlockSpec((tm, D), lambda b, t: (b * nt + t, 0)),
                  pl.BlockSpec((FF_HALO, D), lambda b, t: (jnp.maximum((b * nt + t) * hb - 1, 0), 0)),
                  _const_spec((1, D))] + [_const_spec(w.shape) for w in (wu, dw, db, wd)],
        out_specs=pl.BlockSpec((tm, D), lambda b, t: (b * nt + t, 0)),
        out_shape=jax.ShapeDtypeStruct((R, D), F32),
        scratch_shapes=[pltpu.VMEM((FF_HALO + tm, D), BF16),
                        pltpu.VMEM((2, FF_HALO + tm, FF_CHUNK), F32), pltpu.VMEM((2, FF_HALO + tm, FF_CHUNK), F32),
                        pltpu.VMEM((tm, D), F32)],
        compiler_params=_params(("parallel", "parallel")),
        name="conv_ffn",
    )(h, h, g, wu, dw, db, wd)


def _pack_in_proj(w):
    HW = HG_HEADS * HG_DK
    o = 4 * HW
    cq = w[:, o:o + AT_QRANK]
    o += AT_QRANK
    pieces = [cq]
    for n in (AT_DH, AT_DH, IDX_DIM, IDX_HEADS):
        pieces.append(jnp.pad(w[:, o:o + n], ((0, 0), (0, 128 - n))))
        o += n
    w_at = jnp.concatenate(pieces, axis=1)
    w_cv = w[:, o:o + 2 * CV_CH]
    o += 2 * CV_CH
    w_gt = w[:, o:o + N_BRANCH * D_MODEL]
    return [a.astype(BF16) for a in (w[:, :4 * HW], w_at, w_cv, w_gt)]


def kernel(x, meta_tokens, hgrn_lb, norm1_g, w_in, hg_norm_g, w_hg_out, cq_norm_g, w_uq, w_qi, q_norm_g, k_norm_g, w_at_out, cv_dw_w, cv_dw_b, cv_ln_g, cv_ln_b, w_cv_out, w_mix_out, norm2_g, w_ffn_up, ffn_dw_w, ffn_dw_b, w_ffn_down):
    B, SEQ, D = x.shape
    depth = w_in.shape[0]
    Tp = PADF + SEQ
    R = B * Tp
    n_sel = min(TOPK_MAX, SEQ // 4)

    meta = jnp.broadcast_to(meta_tokens.astype(x.dtype)[None], (B, N_META, D))
    h = jnp.concatenate([jnp.zeros((B, DEAD, D), x.dtype), meta, x], axis=1).reshape(R, D)
    p = jax.nn.softmax(hgrn_lb.astype(F32), axis=0)
    lbs = jnp.cumsum(p, axis=0) - p[0]

    row = lambda a: a.reshape(1, -1).astype(F32)
    for l in range(depth):
        w_hg, w_at, w_cv, w_gt = _pack_in_proj(w_in[l])
        zh, zatt, zu, zg = _in_proj(h, row(norm1_g[l]), w_hg, w_at, w_cv, w_gt)

        o_hg = _hgrn(zh, row(lbs[l]), row(hg_norm_g[l]), B, Tp)
        o_cv = _conv_branch(zu, cv_dw_w[l].astype(F32), row(cv_dw_b[l]), row(cv_ln_g[l]), row(cv_ln_b[l]), B, Tp)

        wuq_t = w_uq[l].reshape(AT_QRANK, AT_HEADS, AT_DH).transpose(1, 2, 0).astype(BF16)
        wqi_t = w_qi[l].reshape(AT_QRANK, IDX_HEADS, IDX_DIM).transpose(1, 2, 0).astype(BF16)
        qt, qit, k, vt, ki, wts = _dsa_prep(zatt, row(cq_norm_g[l]), wuq_t, wqi_t,
                                            q_norm_g[l].reshape(-1, 1).astype(F32), row(k_norm_g[l]))
        wt = jnp.pad(wts[:, :IDX_HEADS].T, ((0, 8 - IDX_HEADS), (0, 0)))
        wo = w_at_out[l].reshape(AT_HEADS, AT_DH, D).astype(BF16)
        y_at = _dsa_main(qt, qit, wt, k, vt, ki, wo, B, Tp, n_sel)

        h = _mix(o_hg, y_at, o_cv, zg, h, w_hg_out[l].astype(BF16), w_cv_out[l].astype(BF16),
                 w_mix_out[l].astype(BF16))

        h = _ffn(h, row(norm2_g[l]), w_ffn_up[l].astype(BF16), ffn_dw_w[l].astype(F32), row(ffn_dw_b[l]),
                 w_ffn_down[l].astype(BF16), B, Tp)

    return h.reshape(B, Tp, D)[:, PADF:]
```

```python
import functools

import numpy as np
import jax
import jax.numpy as jnp
from jax import lax
from jax.experimental import pallas as pl
from jax.experimental.pallas import tpu as pltpu

F32 = jnp.float32
BF16 = jnp.bfloat16
I32 = jnp.int32

D_MODEL = 1024
N_META = 16
HG_HEADS = 4
HG_DK = 128
HG_DV = 128
HG_CHUNK = 64
AT_HEADS = 8
AT_DH = 64
AT_QRANK = 256
IDX_HEADS = 4
IDX_DIM = 64
TOPK_MAX = 256
Q_BLOCK = 128
CV_CH = 512
CV_WIDTH = 31
FF_DIM = 2816
FF_CONV = 3
N_BRANCH = 3
EPS = 1e-6
MASK_BIG = 1e30

PADF = Q_BLOCK
DEAD = PADF - N_META
FF_CHUNK = 256
N_FF_CHUNK = FF_DIM // FF_CHUNK
ATT_SLAB = 768
CV_HALO = 32
FF_HALO = 16
DSA_GROUP = 4
LOG2E = 1.4426950408889634
VT_ROWS = AT_DH + 16
KEY_NEG = int(np.array(-MASK_BIG, np.float32).view(np.int32)) ^ 0x7FFFFFFF
VMEM_LIMIT = 56 * 1024 * 1024

_NT = (((1,), (1,)), ((), ()))
_TN = (((0,), (0,)), ((), ()))


def _pick(n, cands):
    for c in cands:
        if n % c == 0:
            return c
    raise ValueError(f"no tile for {n}")


def _const_spec(shape):
    nd = len(shape)
    return pl.BlockSpec(shape, lambda *_: (0,) * nd, pipeline_mode=pl.Buffered(1))


def _params(sem):
    return pltpu.CompilerParams(dimension_semantics=sem, vmem_limit_bytes=VMEM_LIMIT)


def _sigmoid(x):
    return 1.0 / (1.0 + jnp.exp(-x))


def _rms(x, g):
    return x * lax.rsqrt(jnp.mean(x * x, axis=-1, keepdims=True) + EPS) * g


def _in_proj_body(h_ref, g_ref, w_hg, w_at, w_cv, w_gt, o_hg, o_at, o_cv, o_gt):
    xn = _rms(h_ref[...], g_ref[...]).astype(BF16)
    for w, o in ((w_hg, o_hg), (w_at, o_at), (w_cv, o_cv), (w_gt, o_gt)):
        o[...] = jnp.dot(xn, w[...], preferred_element_type=F32).astype(o.dtype)


def _in_proj(h, g, w_hg, w_at, w_cv, w_gt):
    R, D = h.shape
    tm = _pick(R, (256, 128))
    outs = ((w_hg.shape[1], F32), (w_at.shape[1], F32), (w_cv.shape[1], BF16), (w_gt.shape[1], BF16))
    return pl.pallas_call(
        _in_proj_body,
        grid=(R // tm,),
        in_specs=[pl.BlockSpec((tm, D), lambda i: (i, 0)), _const_spec((1, D))]
        + [_const_spec(w.shape) for w in (w_hg, w_at, w_cv, w_gt)],
        out_specs=[pl.BlockSpec((tm, n), lambda i: (i, 0)) for n, _ in outs],
        out_shape=[jax.ShapeDtypeStruct((R, n), dt) for n, dt in outs],
        compiler_params=_params(("parallel",)),
        name="in_proj",
    )(h, g, w_hg, w_at, w_cv, w_gt)


def _hgrn_consts():
    C = HG_CHUNK
    t = np.arange(C)[:, None]
    s = np.arange(C)[None, :]
    mats = [(s <= t)]
    for m in (8, 16, 32):
        rho = (t // (2 * m)) * (2 * m) + m - 1
        mats.append(s <= rho)
    return np.concatenate(mats, axis=0).astype(np.float32)


def _split3(x):
    hi = x.astype(BF16)
    r1 = x - hi.astype(F32)
    mid = r1.astype(BF16)
    lo = (r1 - mid.astype(F32)).astype(BF16)
    return hi, mid, lo


def _hgrn_body(q_ref, f_ref, i_ref, g_ref, lb_ref, ng_ref, cm_ref, o_ref, st_ref, *, tc):
    C = HG_CHUNK
    it = pl.program_id(1)

    @pl.when(it == 0)
    def _():
        st_ref[...] = jnp.zeros_like(st_ref)

    row = lax.broadcasted_iota(I32, (C, C), 0)
    col = lax.broadcasted_iota(I32, (C, C), 1)
    m_diag = (row // 8 == col // 8) & (col <= row)
    level_masks = []
    for m in (8, 16, 32):
        level_masks.append((row // (2 * m) == col // (2 * m)) & ((row // m) % 2 == 1) & ((col // m) % 2 == 0))
    rid = lax.broadcasted_iota(I32, (C, 1), 0)
    second = [(rid // m) % 2 == 1 for m in (8, 16, 32)]
    lane_c = lax.broadcasted_iota(I32, (8, C), 1)
    cm = cm_ref[...]
    ng = ng_ref[...]

    def chunk(c, carry):
        r0 = pl.multiple_of(c * C, C)
        live = (it * tc + r0 + rid) >= DEAD
        for h in range(HG_HEADS):
            cs = slice(h * HG_DK, (h + 1) * HG_DK)
            lb = lb_ref[:, cs]
            f = lb + (1.0 - lb) * _sigmoid(f_ref[pl.ds(r0, C), cs])
            lf = jnp.where(live, jnp.log(f), 0.0)
            k = jnp.where(live, 1.0 - f, 0.0)
            qr = q_ref[pl.ds(r0, C), cs]
            q = qr * _sigmoid(qr) * (HG_DK ** -0.5)
            v = i_ref[pl.ds(r0, C), cs].astype(BF16)
            cums = sum(jnp.dot(cm, part, preferred_element_type=F32) for part in _split3(lf))
            b = cums[0:C]
            att = jnp.zeros((C, C), F32)
            for lvl in range(3):
                r = cums[(lvl + 1) * C:(lvl + 2) * C]
                e = jnp.exp(jnp.minimum(jnp.where(second[lvl], b - r, r - b), 0.0))
                a = lax.dot_general((q * e).astype(BF16), (k * e).astype(BF16), _NT, preferred_element_type=F32)
                att = jnp.where(level_masks[lvl], a, att)
            blocks = []
            for j in range(C // 8):
                bR = b[8 * j:8 * j + 8]
                qR = q[8 * j:8 * j + 8]
                acc = jnp.zeros((8, C), F32)
                for s in range(8 * j, 8 * j + 8):
                    e = jnp.exp(jnp.minimum(bR - b[s:s + 1], 0.0))
                    colv = jnp.sum(qR * (k[s:s + 1] * e), axis=-1, keepdims=True)
                    acc = jnp.where(lane_c == s, colv, acc)
                blocks.append(acc)
            att = jnp.where(m_diag, jnp.concatenate(blocks, axis=0), att)
            st = st_ref[h]
            o = jnp.dot(att.astype(BF16), v, preferred_element_type=F32)
            o = o + lax.dot_general((q * jnp.exp(b)).astype(BF16), st.astype(BF16), _NT, preferred_element_type=F32)
            bl = b[C - 1:C]
            kb = (k * jnp.exp(bl - b)).astype(BF16)
            st_ref[h] = st * jnp.exp(bl) + lax.dot_general(v, kb, _TN, preferred_element_type=F32)
            gr = g_ref[pl.ds(r0, C), cs]
            o_ref[pl.ds(r0, C), cs] = (_rms(o, ng) * (gr * _sigmoid(gr))).astype(BF16)
        return carry

    lax.fori_loop(0, tc // C, chunk, 0)


def _hgrn(zh, lb, ng, B, Tp):
    R = zh.shape[0]
    W = HG_HEADS * HG_DK
    tc = _pick(Tp, (640, 384, 128))
    nt = Tp // tc
    cm = jnp.asarray(_hgrn_consts(), BF16)

    def col(cb):
        return pl.BlockSpec((tc, W), lambda b, t: (b * nt + t, cb))

    return pl.pallas_call(
        functools.partial(_hgrn_body, tc=tc),
        grid=(B, nt),
        in_specs=[col(0), col(1), col(2), col(3), _const_spec((1, W)), _const_spec((1, HG_DV)),
                  _const_spec(cm.shape)],
        out_specs=pl.BlockSpec((tc, W), lambda b, t: (b * nt + t, 0)),
        out_shape=jax.ShapeDtypeStruct((R, W), BF16),
        scratch_shapes=[pltpu.VMEM((HG_HEADS, HG_DV, HG_DK), F32)],
        compiler_params=_params(("parallel", "arbitrary")),
        name="hgrn2",
    )(zh, zh, zh, zh, lb, ng, cm)


def _conv_body(u_ref, uh_ref, dw_ref, db_ref, lg_ref, lb_ref, o_ref, scr, sh_scr, *, tt):
    it = pl.program_id(1)
    RC = 32
    L = tt + CV_HALO - 8

    def glu(u, pos0):
        u = u.astype(F32)
        hh = u[:, :CV_CH] * _sigmoid(u[:, CV_CH:])
        pos = pos0 + lax.broadcasted_iota(I32, (u.shape[0], 1), 0)
        return jnp.where(pos >= DEAD, hh, 0.0)

    scr[0:CV_HALO] = glu(uh_ref[...], it * tt - CV_HALO)
    scr[CV_HALO:CV_HALO + tt] = glu(u_ref[...], it * tt)
    for s in range(1, 8):
        sh_scr[s - 1] = scr[s:s + L]
    bias = db_ref[...]
    lg = lg_ref[...]
    lbias = lb_ref[...]

    def chunk(c, carry):
        r0 = pl.multiple_of(c * RC, RC)
        acc = jnp.zeros((RC // 8, 8, CV_CH), F32)
        for j in range(CV_WIDTH):
            off = CV_HALO - CV_WIDTH + 1 + j
            base = pl.ds(r0 + (off // 8) * 8, RC)
            xs = scr[base, :] if off % 8 == 0 else sh_scr[off % 8 - 1, base, :]
            acc = acc + dw_ref[j][None] * xs.reshape(RC // 8, 8, CV_CH)
        acc = acc.reshape(RC, CV_CH) + bias
        mu = jnp.mean(acc, axis=-1, keepdims=True)
        xc = acc - mu
        y = xc * lax.rsqrt(jnp.mean(xc * xc, axis=-1, keepdims=True) + EPS) * lg + lbias
        o_ref[pl.ds(r0, RC), :] = (y * _sigmoid(y)).astype(BF16)
        return carry

    lax.fori_loop(0, tt // RC, chunk, 0)


def _conv_branch(zu, dw, db, lg, lb, B, Tp):
    R = zu.shape[0]
    tt = _pick(Tp, (640, 384, 128))
    nt = Tp // tt
    hb = tt // CV_HALO
    return pl.pallas_call(
        functools.partial(_conv_body, tt=tt),
        grid=(B, nt),
        in_specs=[pl.BlockSpec((tt, 2 * CV_CH), lambda b, t: (b * nt + t, 0)),
                  pl.BlockSpec((CV_HALO, 2 * CV_CH), lambda b, t: (jnp.maximum((b * nt + t) * hb - 1, 0), 0)),
                  _const_spec(dw.shape), _const_spec((1, CV_CH)), _const_spec((1, CV_CH)), _const_spec((1, CV_CH))],
        out_specs=pl.BlockSpec((tt, CV_CH), lambda b, t: (b * nt + t, 0)),
        out_shape=jax.ShapeDtypeStruct((R, CV_CH), BF16),
        scratch_shapes=[pltpu.VMEM((CV_HALO + tt, CV_CH), F32), pltpu.VMEM((7, CV_HALO + tt - 8, CV_CH), F32)],
        compiler_params=_params(("parallel", "parallel")),
        name="conformer_conv",
    )(zu, zu, dw, db, lg, lb)


def _dsa_prep_body(z_ref, cg_ref, wuq_ref, wqi_ref, qg_ref, kg_ref, eye_ref,
                   q_ref, qi_ref, k_ref, vt_ref, ki_ref, w_ref):
    z = z_ref[...]
    cqn = _rms(z[:, 0:AT_QRANK], cg_ref[...]).astype(BF16)
    qg = qg_ref[...]
    for h in range(AT_HEADS):
        qh = lax.dot_general(wuq_ref[h], cqn, _NT, preferred_element_type=F32)
        qh = qh * lax.rsqrt(jnp.mean(qh * qh, axis=0, keepdims=True) + EPS) * qg
        q_ref[h] = (qh * (AT_DH ** -0.5 * LOG2E)).astype(BF16)
    for h in range(IDX_HEADS):
        qi_ref[h] = lax.dot_general(wqi_ref[h], cqn, _NT, preferred_element_type=F32).astype(BF16)
    k_ref[...] = _rms(z[:, 256:256 + AT_DH], kg_ref[...]).astype(BF16)
    vt_ref[...] = lax.dot_general(eye_ref[...], z[:, 384:384 + AT_DH].astype(BF16), _NT,
                                  preferred_element_type=F32).astype(BF16)
    ki_ref[...] = z[:, 512:512 + IDX_DIM].astype(BF16)
    w_ref[...] = z[:, 640:768].T[0:8, :] * ((IDX_HEADS * IDX_DIM) ** -0.5)


def _dsa_prep(zatt, cg, wuq_t, wqi_t, qg_col, kg):
    R = zatt.shape[0]
    tm = _pick(R, (512, 256, 128))
    row64 = pl.BlockSpec((tm, 64), lambda i: (i, 0))
    eye = jnp.eye(AT_DH, dtype=BF16)
    return pl.pallas_call(
        _dsa_prep_body,
        grid=(R // tm,),
        in_specs=[pl.BlockSpec((tm, ATT_SLAB), lambda i: (i, 0)), _const_spec((1, AT_QRANK)),
                  _const_spec(wuq_t.shape), _const_spec(wqi_t.shape), _const_spec((AT_DH, 1)),
                  _const_spec((1, AT_DH)), _const_spec(eye.shape)],
        out_specs=[pl.BlockSpec((AT_HEADS, AT_DH, tm), lambda i: (0, 0, i)),
                   pl.BlockSpec((IDX_HEADS, IDX_DIM, tm), lambda i: (0, 0, i)),
                   row64, pl.BlockSpec((AT_DH, tm), lambda i: (0, i)), row64,
                   pl.BlockSpec((8, tm), lambda i: (0, i))],
        out_shape=[jax.ShapeDtypeStruct((AT_HEADS, AT_DH, R), BF16),
                   jax.ShapeDtypeStruct((IDX_HEADS, IDX_DIM, R), BF16),
                   jax.ShapeDtypeStruct((R, AT_DH), BF16), jax.ShapeDtypeStruct((AT_DH, R), BF16),
                   jax.ShapeDtypeStruct((R, IDX_DIM), BF16), jax.ShapeDtypeStruct((8, R), F32)],
        compiler_params=_params(("parallel",)),
        name="dsa_prep",
    )(zatt, cg, wuq_t, wqi_t, qg_col, kg, eye)


def _dsa_body(qt_ref, qit_ref, wt_ref, k_ref, vt_ref, ki_ref, wo_ref, tri_ref, o_ref,
              sc_ref, hi_ref, lo_ref, kaug_ref, qaug_ref, s_ref, p_ref, m_ref, acc_ref, *, n_sel):
    QB = Q_BLOCK
    G = DSA_GROUP
    I16 = jnp.int16
    qb = pl.program_id(1)
    ngrp = (qb + G) // G
    keyi = lax.broadcasted_iota(I32, (QB, QB), 0)
    qpos = qb * QB + lax.broadcasted_iota(I32, (QB, QB), 1)

    qits = [qit_ref[h] for h in range(IDX_HEADS)]
    wrows = [wt_ref[h:h + 1, :] for h in range(IDX_HEADS)]

    def score_grp(g, masked):
        for u in range(G):
            j = g * G + u
            kij = ki_ref[pl.ds(pl.multiple_of(j * QB, QB), QB), :]
            tot = jnp.zeros((QB, QB), F32)
            for h in range(IDX_HEADS):
                s = jnp.dot(kij, qits[h], preferred_element_type=F32)
                tot = tot + jnp.maximum(s, 0.0) * wrows[h]
            if masked:
                kpos = j * QB + keyi
                tot = jnp.where(kpos < PADF, MASK_BIG, tot)
                tot = jnp.where((kpos <= qpos) & (kpos >= DEAD), tot, -MASK_BIG)
            bits = pltpu.bitcast(tot + 0.0, I32)
            key = bits ^ ((bits >> 31) & 0x7FFFFFFF)
            sc_ref[j] = key
            hi_ref[j] = (key >> 16).astype(I16)
            lo_ref[j] = ((key & 0xFFFF) - 32768).astype(I16)

    score_grp(0, True)

    def interior(g, carry):
        score_grp(g, False)
        return carry

    lax.fori_loop(1, ngrp - 1, interior, 0)

    @pl.when(ngrp > 1)
    def _():
        score_grp(ngrp - 1, True)

    @pl.when(ngrp % 2 == 1)
    def _():
        for u in range(G):
            sc_ref[ngrp * G + u] = jnp.full((QB, QB), KEY_NEG, I32)
            hi_ref[ngrp * G + u] = jnp.full((QB, QB), -32768, I16)
            lo_ref[ngrp * G + u] = jnp.full((QB, QB), -32768, I16)

    one = jnp.ones((QB, QB), I16)
    zero = jnp.zeros((QB, QB), I16)

    def count16(ref, pred):
        def tree(xs):
            while len(xs) > 1:
                xs = [a + b for a, b in zip(xs[0::2], xs[1::2])]
            return xs[0]

        def cb(g, a):
            parts = []
            for u in range(2 * G):
                x = jnp.where(pred(ref[g * (2 * G) + u]), one, zero)
                parts.append(tree([x[16 * r:16 * r + 16] for r in range(QB // 16)]))
            return a + tree(parts)
        a = lax.fori_loop(0, (ngrp + 1) // 2, cb, jnp.zeros((16, QB), I16))
        return jnp.sum(a.astype(I32), axis=0, keepdims=True)

    def radix16(ref, want):
        def bit_body(bi, c):
            trial = c + lax.shift_left(jnp.int32(1), 15 - bi)
            t16 = trial.astype(I16)
            cnt = count16(ref, lambda x: x >= t16)
            return jnp.where(cnt >= want, trial, c)
        return lax.fori_loop(0, 16, bit_body, jnp.full((1, QB), -32768, I32))

    c_hi = radix16(hi_ref, n_sel)
    c_hi16 = c_hi.astype(I16)
    want_lo = n_sel - count16(hi_ref, lambda x: x > c_hi16)

    def band_grp(g, carry):
        for u in range(G):
            j = g * G + u
            lo_ref[j] = jnp.where(hi_ref[j] == c_hi16, lo_ref[j], jnp.full((QB, QB), -32768, I16))
        return carry

    lax.fori_loop(0, ngrp, band_grp, 0)
    c_lo = radix16(lo_ref, want_lo)
    c_lo16 = c_lo.astype(I16)
    c = (c_hi << 16) | (c_lo + 32768)
    need = (want_lo - count16(lo_ref, lambda x: x > c_lo16)).astype(F32)

    m_ref[...] = jnp.full(m_ref.shape, -3.0e38, F32)
    acc_ref[...] = jnp.zeros(acc_ref.shape, F32)
    tri = tri_ref[...]
    need = jnp.where(c > KEY_NEG, need, -1.0)

    rr = lax.broadcasted_iota(I32, (QB, 2 * QB), 0)
    cc = lax.broadcasted_iota(I32, (QB, 2 * QB), 1)
    eye2 = jnp.where((cc == rr) | (cc == rr + QB), 1.0, 0.0).astype(BF16)
    for hp in range(AT_HEADS // 2):
        qaug_ref[hp, 0:QB, :] = eye2
        qaug_ref[hp, QB:QB + AT_DH, 0:QB] = qt_ref[2 * hp]
        qaug_ref[hp, QB:QB + AT_DH, QB:2 * QB] = qt_ref[2 * hp + 1]

    def scores(g, eqc, slot):
        for u in range(G):
            key = sc_ref[g * G + u]
            eqf = jnp.where(key == c, 1.0, 0.0)
            pref = jnp.dot(tri, eqf.astype(BF16), preferred_element_type=F32)
            sel = (key > c) | ((key == c) & (pref <= need - eqc))
            eqc = eqc + jnp.sum(eqf, axis=0, keepdims=True)
            kaug_ref[slot, u * QB:(u + 1) * QB, 0:QB] = jnp.where(sel, 0.0, -MASK_BIG).astype(BF16)
        r0 = pl.multiple_of(g * (G * QB), G * QB)
        kaug_ref[slot, :, QB:QB + AT_DH] = k_ref[pl.ds(r0, G * QB), :]
        kaug = kaug_ref[slot]
        for hp in range(AT_HEADS // 2):
            s_ref[slot, hp] = jnp.dot(kaug, qaug_ref[hp], preferred_element_type=F32)
        return eqc

    def attend(g, slot):
        r0 = pl.multiple_of(g * (G * QB), G * QB)
        vtg = vt_ref[:, pl.ds(r0, G * QB)]
        alphas = []
        for hp in range(AT_HEADS // 2):
            hs = slice(hp * 2 * QB, (hp + 1) * 2 * QB)
            m_old = m_ref[:, hs]
            m_new = jnp.maximum(m_old, jnp.max(s_ref[slot, hp], axis=0, keepdims=True))
            m_ref[:, hs] = m_new
            p_ref[slot, hp] = jnp.exp2(s_ref[slot, hp] - m_new).astype(BF16)
            alphas.append(jnp.exp2(m_old - m_new))
        for hp in range(AT_HEADS // 2):
            hs = slice(hp * 2 * QB, (hp + 1) * 2 * QB)
            acc_ref[:, hs] = alphas[hp] * acc_ref[:, hs] + jnp.dot(vtg, p_ref[slot, hp],
                                                                   preferred_element_type=F32)

    npair = (ngrp + 1) // 2
    last = 2 * npair - 1

    def att_pair(i, eqc):
        eqc = scores(2 * i + 1, eqc, 1)
        attend(2 * i, 0)
        eqc_next = scores(jnp.minimum(2 * i + 2, last), eqc, 0)
        attend(2 * i + 1, 1)
        return eqc_next

    lax.fori_loop(0, npair, att_pair, scores(0, jnp.zeros((1, QB), F32), 0))

    ot = (acc_ref[0:AT_DH, :] / acc_ref[AT_DH:AT_DH + 1, :]).astype(BF16)
    y = jnp.zeros((QB, D_MODEL), F32)
    for h in range(AT_HEADS):
        y = y + lax.dot_general(ot[:, h * QB:(h + 1) * QB], wo_ref[h], _TN, preferred_element_type=F32)
    o_ref[...] = y.astype(BF16)


def _dsa_main(qt, qit, wt, k, vt, ki, wo, B, Tp, n_sel):
    R = k.shape[0]
    QB = Q_BLOCK
    nb = Tp // QB
    nbc = -(-nb // (2 * DSA_GROUP)) * (2 * DSA_GROUP)
    assert n_sel <= DSA_GROUP * QB
    Tx = nbc * QB
    pad_rows = lambda a: jnp.pad(a.reshape(B, Tp, -1), ((0, 0), (0, Tx - Tp), (0, 0))).reshape(B * Tx, -1)
    k, ki = pad_rows(k), pad_rows(ki)
    vt = jnp.pad(vt.reshape(AT_DH, B, Tp), ((0, 0), (0, 0), (0, Tx - Tp))).reshape(AT_DH, B * Tx)
    vt = jnp.concatenate([vt, jnp.ones((1, B * Tx), BF16), jnp.zeros((VT_ROWS - AT_DH - 1, B * Tx), BF16)], axis=0)
    tri = jnp.asarray(np.tril(np.ones((QB, QB), np.float32)), BF16)
    seq = lambda: pl.BlockSpec((Tx, 64), lambda b, i: (b, 0))
    return pl.pallas_call(
        functools.partial(_dsa_body, n_sel=n_sel),
        grid=(B, nb),
        in_specs=[pl.BlockSpec((AT_HEADS, AT_DH, QB), lambda b, i: (0, 0, b * nb + i)),
                  pl.BlockSpec((IDX_HEADS, IDX_DIM, QB), lambda b, i: (0, 0, b * nb + i)),
                  pl.BlockSpec((8, QB), lambda b, i: (0, b * nb + i)),
                  seq(), pl.BlockSpec((VT_ROWS, Tx), lambda b, i: (0, b)), seq(),
                  _const_spec(wo.shape), _const_spec(tri.shape)],
        out_specs=pl.BlockSpec((QB, D_MODEL), lambda b, i: (b * nb + i, 0)),
        out_shape=jax.ShapeDtypeStruct((R, D_MODEL), BF16),
        scratch_shapes=[pltpu.VMEM((nbc, QB, QB), I32), pltpu.VMEM((nbc, QB, QB), jnp.int16),
                        pltpu.VMEM((nbc, QB, QB), jnp.int16),
                        pltpu.VMEM((2, DSA_GROUP * QB, QB + AT_DH), BF16),
                        pltpu.VMEM((AT_HEADS // 2, QB + AT_DH, 2 * QB), BF16),
                        pltpu.VMEM((2, AT_HEADS // 2, DSA_GROUP * QB, 2 * QB), F32),
                        pltpu.VMEM((2, AT_HEADS // 2, DSA_GROUP * QB, 2 * QB), BF16),
                        pltpu.VMEM((1, AT_HEADS * QB), F32), pltpu.VMEM((VT_ROWS, AT_HEADS * QB), F32)],
        compiler_params=_params(("parallel", "arbitrary")),
        name="dsa_attention",
    )(qt, qit, wt, k, vt, ki, wo, tri)


def _mix_body(ohg_ref, yat_ref, ocv_ref, gt_ref, h_ref, whg_ref, wcv_ref, wmix_ref, o_ref):
    y_hg = jnp.dot(ohg_ref[...], whg_ref[...], preferred_element_type=F32)
    y_cv = jnp.dot(ocv_ref[...], wcv_ref[...], preferred_element_type=F32)
    y_at = yat_ref[...].astype(F32)
    g1 = _sigmoid(gt_ref[:, 0:D_MODEL].astype(F32))
    g2 = _sigmoid(gt_ref[:, D_MODEL:2 * D_MODEL].astype(F32))
    g3 = _sigmoid(gt_ref[:, 2 * D_MODEL:3 * D_MODEL].astype(F32))
    mixed = (g1 * y_hg + g2 * y_at + g3 * y_cv).astype(BF16)
    o_ref[...] = h_ref[...] + jnp.dot(mixed, wmix_ref[...], preferred_element_type=F32)


def _mix(ohg, yat, ocv, gates, h, whg, wcv, wmix):
    R, D = h.shape
    tm = _pick(R, (512, 256, 128))
    rows = lambda n: pl.BlockSpec((tm, n), lambda i: (i, 0))
    return pl.pallas_call(
        _mix_body,
        grid=(R // tm,),
        in_specs=[rows(ohg.shape[1]), rows(D), rows(ocv.shape[1]), rows(3 * D), rows(D),
                  _const_spec(whg.shape), _const_spec(wcv.shape), _const_spec(wmix.shape)],
        out_specs=rows(D),
        out_shape=jax.ShapeDtypeStruct((R, D), F32),
        compiler_params=_params(("parallel",)),
        name="branch_mix",
    )(ohg, yat, ocv, gates, h, whg, wcv, wmix)


def _ffn_body(h_ref, hh_ref, g_ref, wu_ref, dw_ref, db_ref, wd_ref, o_ref,
              xn_scr, ua_scr, ub_scr, acc_scr, *, tm):
    it = pl.program_id(1)
    g = g_ref[...]
    pos = it * tm - FF_HALO + lax.broadcasted_iota(I32, (tm + FF_HALO, 1), 0)
    live = pos >= DEAD
    xn_scr[0:FF_HALO] = jnp.where(live[0:FF_HALO], _rms(hh_ref[...], g), 0.0).astype(BF16)
    xn_scr[FF_HALO:FF_HALO + tm] = jnp.where(live[FF_HALO:], _rms(h_ref[...], g), 0.0).astype(BF16)
    acc_scr[...] = jnp.zeros(acc_scr.shape, F32)

    def conv(scr, slot, col0):
        cols = slice(col0, col0 + FF_CHUNK)
        y = db_ref[:, cols]
        for j in range(FF_CONV):
            r = FF_HALO - (FF_CONV - 1) + j
            y = y + dw_ref[j:j + 1, cols] * scr[slot, r:r + tm]
        return y

    xn = xn_scr[...]
    for c in range(N_FF_CHUNK):
        slot = c % 2
        ca, cb = c * FF_CHUNK, FF_DIM + c * FF_CHUNK
        ua_scr[slot] = jnp.dot(xn, wu_ref[:, ca:ca + FF_CHUNK], preferred_element_type=F32)
        ub_scr[slot] = jnp.dot(xn, wu_ref[:, cb:cb + FF_CHUNK], preferred_element_type=F32)
        a = conv(ua_scr, slot, ca)
        b = conv(ub_scr, slot, cb)
        act = (a * _sigmoid(a) * b).astype(BF16)
        acc_scr[...] += jnp.dot(act, wd_ref[ca:ca + FF_CHUNK, :], preferred_element_type=F32)
    o_ref[...] = h_ref[...] + acc_scr[...]


def _ffn(h, g, wu, dw, db, wd, B, Tp):
    R, D = h.shape
    tm = _pick(Tp, (640, 384, 128))
    nt = Tp // tm
    hb = tm // FF_HALO
    return pl.pallas_call(
        functools.partial(_ffn_body, tm=tm),
        grid=(B, nt),
        in_specs=[pl.BlockSpec((tm, D), lambda b, t: (b * nt + t, 0)),
                  pl.BlockSpec((FF_HALO, D), lambda b, t: (jnp.maximum((b * nt + t) * hb - 1, 0), 0)),
                  _const_spec((1, D))] + [_const_spec(w.shape) for w in (wu, dw, db, wd)],
        out_specs=pl.BlockSpec((tm, D), lambda b, t: (b * nt + t, 0)),
        out_shape=jax.ShapeDtypeStruct((R, D), F32),
        scratch_shapes=[pltpu.VMEM((FF_HALO + tm, D), BF16),
                        pltpu.VMEM((2, FF_HALO + tm, FF_CHUNK), F32), pltpu.VMEM((2, FF_HALO + tm, FF_CHUNK), F32),
                        pltpu.VMEM((tm, D), F32)],
        compiler_params=_params(("parallel", "parallel")),
        name="conv_ffn",
    )(h, h, g, wu, dw, db, wd)


def _pack_in_proj(w):
    HW = HG_HEADS * HG_DK
    o = 4 * HW
    cq = w[:, o:o + AT_QRANK]
    o += AT_QRANK
    pieces = [cq]
    for n in (AT_DH, AT_DH, IDX_DIM, IDX_HEADS):
        pieces.append(jnp.pad(w[:, o:o + n], ((0, 0), (0, 128 - n))))
        o += n
    w_at = jnp.concatenate(pieces, axis=1)
    w_cv = w[:, o:o + 2 * CV_CH]
    o += 2 * CV_CH
    w_gt = w[:, o:o + N_BRANCH * D_MODEL]
    return [a.astype(BF16) for a in (w[:, :4 * HW], w_at, w_cv, w_gt)]


def kernel(x, meta_tokens, hgrn_lb, norm1_g, w_in, hg_norm_g, w_hg_out, cq_norm_g, w_uq, w_qi, q_norm_g, k_norm_g, w_at_out, cv_dw_w, cv_dw_b, cv_ln_g, cv_ln_b, w_cv_out, w_mix_out, norm2_g, w_ffn_up, ffn_dw_w, ffn_dw_b, w_ffn_down):
    B, SEQ, D = x.shape
    depth = w_in.shape[0]
    Tp = PADF + SEQ
    R = B * Tp
    n_sel = min(TOPK_MAX, SEQ // 4)

    meta = jnp.broadcast_to(meta_tokens.astype(x.dtype)[None], (B, N_META, D))
    h = jnp.concatenate([jnp.zeros((B, DEAD, D), x.dtype), meta, x], axis=1).reshape(R, D)
    p = jax.nn.softmax(hgrn_lb.astype(F32), axis=0)
    lbs = jnp.cumsum(p, axis=0) - p[0]

    row = lambda a: a.reshape(1, -1).astype(F32)
    for l in range(depth):
        w_hg, w_at, w_cv, w_gt = _pack_in_proj(w_in[l])
        zh, zatt, zu, zg = _in_proj(h, row(norm1_g[l]), w_hg, w_at, w_cv, w_gt)

        o_hg = _hgrn(zh, row(lbs[l]), row(hg_norm_g[l]), B, Tp)
        dw8 = jnp.broadcast_to(cv_dw_w[l].astype(F32)[:, None, :], (CV_WIDTH, 8, CV_CH))
        o_cv = _conv_branch(zu, dw8, row(cv_dw_b[l]), row(cv_ln_g[l]), row(cv_ln_b[l]), B, Tp)

        wuq_t = w_uq[l].reshape(AT_QRANK, AT_HEADS, AT_DH).transpose(1, 2, 0).astype(BF16)
        wqi_t = w_qi[l].reshape(AT_QRANK, IDX_HEADS, IDX_DIM).transpose(1, 2, 0).astype(BF16)
        qt, qit, k, vt, ki, wt = _dsa_prep(zatt, row(cq_norm_g[l]), wuq_t, wqi_t,
                                           q_norm_g[l].reshape(-1, 1).astype(F32), row(k_norm_g[l]))
        wo = w_at_out[l].reshape(AT_HEADS, AT_DH, D).astype(BF16)
        y_at = _dsa_main(qt, qit, wt, k, vt, ki, wo, B, Tp, n_sel)

        h = _mix(o_hg, y_at, o_cv, zg, h, w_hg_out[l].astype(BF16), w_cv_out[l].astype(BF16),
                 w_mix_out[l].astype(BF16))

        h = _ffn(h, row(norm2_g[l]), w_ffn_up[l].astype(BF16), ffn_dw_w[l].astype(F32), row(ffn_dw_b[l]),
                 w_ffn_down[l].astype(BF16), B, Tp)

    return h.reshape(B, Tp, D)[:, PADF:]
```

```python
import functools

import numpy as np
import jax
import jax.numpy as jnp
from jax import lax
from jax.experimental import pallas as pl
from jax.experimental.pallas import tpu as pltpu

F32 = jnp.float32
BF16 = jnp.bfloat16
I32 = jnp.int32

D_MODEL = 1024
N_META = 16
HG_HEADS = 4
HG_DK = 128
HG_DV = 128
HG_CHUNK = 64
AT_HEADS = 8
AT_DH = 64
AT_QRANK = 256
IDX_HEADS = 4
IDX_DIM = 64
TOPK_MAX = 256
Q_BLOCK = 128
CV_CH = 512
CV_WIDTH = 31
FF_DIM = 2816
FF_CONV = 3
N_BRANCH = 3
EPS = 1e-6
MASK_BIG = 1e30

PADF = Q_BLOCK
DEAD = PADF - N_META
FF_CHUNK = 256
N_FF_CHUNK = FF_DIM // FF_CHUNK
ATT_SLAB = 768
CV_HALO = 32
FF_HALO = 16
DSA_GROUP = 4
LOG2E = 1.4426950408889634
VT_ROWS = AT_DH + 16
KEY_NEG = int(np.array(-MASK_BIG, np.float32).view(np.int32)) ^ 0x7FFFFFFF
VMEM_LIMIT = 56 * 1024 * 1024

_NT = (((1,), (1,)), ((), ()))
_TN = (((0,), (0,)), ((), ()))


def _pick(n, cands):
    for c in cands:
        if n % c == 0:
            return c
    raise ValueError(f"no tile for {n}")


def _const_spec(shape):
    nd = len(shape)
    return pl.BlockSpec(shape, lambda *_: (0,) * nd, pipeline_mode=pl.Buffered(1))


def _params(sem):
    return pltpu.CompilerParams(dimension_semantics=sem, vmem_limit_bytes=VMEM_LIMIT)


def _sigmoid(x):
    return 1.0 / (1.0 + jnp.exp(-x))


def _rms(x, g):
    return x * lax.rsqrt(jnp.mean(x * x, axis=-1, keepdims=True) + EPS) * g


def _in_proj_body(h_ref, g_ref, w_hg, w_at, w_cv, w_gt, o_hg, o_at, o_cv, o_gt):
    xn = _rms(h_ref[...], g_ref[...]).astype(BF16)
    for w, o in ((w_hg, o_hg), (w_at, o_at), (w_cv, o_cv), (w_gt, o_gt)):
        o[...] = jnp.dot(xn, w[...], preferred_element_type=F32).astype(o.dtype)


def _in_proj(h, g, w_hg, w_at, w_cv, w_gt):
    R, D = h.shape
    tm = _pick(R, (256, 128))
    outs = ((w_hg.shape[1], F32), (w_at.shape[1], F32), (w_cv.shape[1], BF16), (w_gt.shape[1], BF16))
    return pl.pallas_call(
        _in_proj_body,
        grid=(R // tm,),
        in_specs=[pl.BlockSpec((tm, D), lambda i: (i, 0)), _const_spec((1, D))]
        + [_const_spec(w.shape) for w in (w_hg, w_at, w_cv, w_gt)],
        out_specs=[pl.BlockSpec((tm, n), lambda i: (i, 0)) for n, _ in outs],
        out_shape=[jax.ShapeDtypeStruct((R, n), dt) for n, dt in outs],
        compiler_params=_params(("parallel",)),
        name="in_proj",
    )(h, g, w_hg, w_at, w_cv, w_gt)


def _hgrn_consts():
    C = HG_CHUNK
    t = np.arange(C)[:, None]
    s = np.arange(C)[None, :]
    mats = [(s <= t)]
    for m in (8, 16, 32):
        rho = (t // (2 * m)) * (2 * m) + m - 1
        mats.append(s <= rho)
    return np.concatenate(mats, axis=0).astype(np.float32)


def _split3(x):
    hi = x.astype(BF16)
    r1 = x - hi.astype(F32)
    mid = r1.astype(BF16)
    lo = (r1 - mid.astype(F32)).astype(BF16)
    return hi, mid, lo


def _hgrn_body(q_ref, f_ref, i_ref, g_ref, lb_ref, ng_ref, cm_ref, o_ref, st_ref, *, tc):
    C = HG_CHUNK
    it = pl.program_id(1)

    @pl.when(it == 0)
    def _():
        st_ref[...] = jnp.zeros_like(st_ref)

    row = lax.broadcasted_iota(I32, (C, C), 0)
    col = lax.broadcasted_iota(I32, (C, C), 1)
    m_diag = (row // 8 == col // 8) & (col <= row)
    level_masks = []
    for m in (8, 16, 32):
        level_masks.append((row // (2 * m) == col // (2 * m)) & ((row // m) % 2 == 1) & ((col // m) % 2 == 0))
    rid = lax.broadcasted_iota(I32, (C, 1), 0)
    second = [(rid // m) % 2 == 1 for m in (8, 16, 32)]
    lane_c = lax.broadcasted_iota(I32, (8, C), 1)
    cm = cm_ref[...]
    ng = ng_ref[...]

    def chunk(c, carry):
        r0 = pl.multiple_of(c * C, C)
        live = (it * tc + r0 + rid) >= DEAD
        for h in range(HG_HEADS):
            cs = slice(h * HG_DK, (h + 1) * HG_DK)
            lb = lb_ref[:, cs]
            f = lb + (1.0 - lb) * _sigmoid(f_ref[pl.ds(r0, C), cs])
            lf = jnp.where(live, jnp.log2(f), 0.0)
            k = jnp.where(live, 1.0 - f, 0.0)
            qr = q_ref[pl.ds(r0, C), cs]
            q = qr * _sigmoid(qr) * (HG_DK ** -0.5)
            v = i_ref[pl.ds(r0, C), cs].astype(BF16)
            cums = sum(jnp.dot(cm, part, preferred_element_type=F32) for part in _split3(lf))
            b = cums[0:C]
            att = jnp.zeros((C, C), F32)
            for lvl in range(3):
                r = cums[(lvl + 1) * C:(lvl + 2) * C]
                e = jnp.exp2(jnp.minimum(jnp.where(second[lvl], b - r, r - b), 0.0))
                a = lax.dot_general((q * e).astype(BF16), (k * e).astype(BF16), _NT, preferred_element_type=F32)
                att = jnp.where(level_masks[lvl], a, att)
            blocks = []
            for j in range(C // 8):
                bR = b[8 * j:8 * j + 8]
                qR = q[8 * j:8 * j + 8]
                acc = jnp.zeros((8, C), F32)
                for s in range(8 * j, 8 * j + 8):
                    e = jnp.exp2(jnp.minimum(bR - b[s:s + 1], 0.0))
                    colv = jnp.sum(qR * (k[s:s + 1] * e), axis=-1, keepdims=True)
                    acc = jnp.where(lane_c == s, colv, acc)
                blocks.append(acc)
            att = jnp.where(m_diag, jnp.concatenate(blocks, axis=0), att)
            st = st_ref[h]
            o = jnp.dot(att.astype(BF16), v, preferred_element_type=F32)
            o = o + lax.dot_general((q * jnp.exp2(b)).astype(BF16), st.astype(BF16), _NT, preferred_element_type=F32)
            bl = b[C - 1:C]
            kb = (k * jnp.exp2(bl - b)).astype(BF16)
            st_ref[h] = st * jnp.exp2(bl) + lax.dot_general(v, kb, _TN, preferred_element_type=F32)
            gr = g_ref[pl.ds(r0, C), cs]
            o_ref[pl.ds(r0, C), cs] = (_rms(o, ng) * (gr * _sigmoid(gr))).astype(BF16)
        return carry

    lax.fori_loop(0, tc // C, chunk, 0)


def _hgrn(zh, lb, ng, B, Tp):
    R = zh.shape[0]
    W = HG_HEADS * HG_DK
    tc = _pick(Tp, (640, 384, 128))
    nt = Tp // tc
    cm = jnp.asarray(_hgrn_consts(), BF16)

    def col(cb):
        return pl.BlockSpec((tc, W), lambda b, t: (b * nt + t, cb))

    return pl.pallas_call(
        functools.partial(_hgrn_body, tc=tc),
        grid=(B, nt),
        in_specs=[col(0), col(1), col(2), col(3), _const_spec((1, W)), _const_spec((1, HG_DV)),
                  _const_spec(cm.shape)],
        out_specs=pl.BlockSpec((tc, W), lambda b, t: (b * nt + t, 0)),
        out_shape=jax.ShapeDtypeStruct((R, W), BF16),
        scratch_shapes=[pltpu.VMEM((HG_HEADS, HG_DV, HG_DK), F32)],
        compiler_params=_params(("parallel", "arbitrary")),
        name="hgrn2",
    )(zh, zh, zh, zh, lb, ng, cm)


def _conv_body(u_ref, uh_ref, dw_ref, db_ref, lg_ref, lb_ref, o_ref, scr, sh_scr, *, tt):
    it = pl.program_id(1)
    RC = 32
    L = tt + CV_HALO - 8

    def glu(u, pos0):
        u = u.astype(F32)
        hh = u[:, :CV_CH] * _sigmoid(u[:, CV_CH:])
        pos = pos0 + lax.broadcasted_iota(I32, (u.shape[0], 1), 0)
        return jnp.where(pos >= DEAD, hh, 0.0)

    scr[0:CV_HALO] = glu(uh_ref[...], it * tt - CV_HALO)
    scr[CV_HALO:CV_HALO + tt] = glu(u_ref[...], it * tt)
    for s in range(1, 8):
        sh_scr[s - 1] = scr[s:s + L]
    bias = db_ref[...]
    lg = lg_ref[...]
    lbias = lb_ref[...]

    def chunk(c, carry):
        r0 = pl.multiple_of(c * RC, RC)
        acc = jnp.zeros((RC // 8, 8, CV_CH), F32)
        for j in range(CV_WIDTH):
            off = CV_HALO - CV_WIDTH + 1 + j
            base = pl.ds(r0 + (off // 8) * 8, RC)
            xs = scr[base, :] if off % 8 == 0 else sh_scr[off % 8 - 1, base, :]
            acc = acc + dw_ref[j][None] * xs.reshape(RC // 8, 8, CV_CH)
        acc = acc.reshape(RC, CV_CH) + bias
        mu = jnp.mean(acc, axis=-1, keepdims=True)
        xc = acc - mu
        y = xc * lax.rsqrt(jnp.mean(xc * xc, axis=-1, keepdims=True) + EPS) * lg + lbias
        o_ref[pl.ds(r0, RC), :] = (y * _sigmoid(y)).astype(BF16)
        return carry

    lax.fori_loop(0, tt // RC, chunk, 0)


def _conv_branch(zu, dw, db, lg, lb, B, Tp):
    R = zu.shape[0]
    tt = _pick(Tp, (640, 384, 128))
    nt = Tp // tt
    hb = tt // CV_HALO
    return pl.pallas_call(
        functools.partial(_conv_body, tt=tt),
        grid=(B, nt),
        in_specs=[pl.BlockSpec((tt, 2 * CV_CH), lambda b, t: (b * nt + t, 0)),
                  pl.BlockSpec((CV_HALO, 2 * CV_CH), lambda b, t: (jnp.maximum((b * nt + t) * hb - 1, 0), 0)),
                  _const_spec(dw.shape), _const_spec((1, CV_CH)), _const_spec((1, CV_CH)), _const_spec((1, CV_CH))],
        out_specs=pl.BlockSpec((tt, CV_CH), lambda b, t: (b * nt + t, 0)),
        out_shape=jax.ShapeDtypeStruct((R, CV_CH), BF16),
        scratch_shapes=[pltpu.VMEM((CV_HALO + tt, CV_CH), F32), pltpu.VMEM((7, CV_HALO + tt - 8, CV_CH), F32)],
        compiler_params=_params(("parallel", "parallel")),
        name="conformer_conv",
    )(zu, zu, dw, db, lg, lb)


def _dsa_prep_body(z_ref, cg_ref, wuq_ref, wqi_ref, qg_ref, kg_ref, eye_ref,
                   q_ref, qi_ref, k_ref, vt_ref, ki_ref, w_ref):
    z = z_ref[...]
    cqn = _rms(z[:, 0:AT_QRANK], cg_ref[...]).astype(BF16)
    qg = qg_ref[...]
    for h in range(AT_HEADS):
        qh = lax.dot_general(wuq_ref[h], cqn, _NT, preferred_element_type=F32)
        qh = qh * lax.rsqrt(jnp.mean(qh * qh, axis=0, keepdims=True) + EPS) * qg
        q_ref[h] = (qh * (AT_DH ** -0.5 * LOG2E)).astype(BF16)
    for h in range(IDX_HEADS):
        qi_ref[h] = lax.dot_general(wqi_ref[h], cqn, _NT, preferred_element_type=F32).astype(BF16)
    k_ref[...] = _rms(z[:, 256:256 + AT_DH], kg_ref[...]).astype(BF16)
    vt_ref[...] = lax.dot_general(eye_ref[...], z[:, 384:384 + AT_DH].astype(BF16), _NT,
                                  preferred_element_type=F32).astype(BF16)
    ki_ref[...] = z[:, 512:512 + IDX_DIM].astype(BF16)
    w_ref[...] = z[:, 640:768].T[0:8, :] * ((IDX_HEADS * IDX_DIM) ** -0.5)


def _dsa_prep(zatt, cg, wuq_t, wqi_t, qg_col, kg):
    R = zatt.shape[0]
    tm = _pick(R, (512, 256, 128))
    row64 = pl.BlockSpec((tm, 64), lambda i: (i, 0))
    eye = jnp.eye(AT_DH, dtype=BF16)
    return pl.pallas_call(
        _dsa_prep_body,
        grid=(R // tm,),
        in_specs=[pl.BlockSpec((tm, ATT_SLAB), lambda i: (i, 0)), _const_spec((1, AT_QRANK)),
                  _const_spec(wuq_t.shape), _const_spec(wqi_t.shape), _const_spec((AT_DH, 1)),
                  _const_spec((1, AT_DH)), _const_spec(eye.shape)],
        out_specs=[pl.BlockSpec((AT_HEADS, AT_DH, tm), lambda i: (0, 0, i)),
                   pl.BlockSpec((IDX_HEADS, IDX_DIM, tm), lambda i: (0, 0, i)),
                   row64, pl.BlockSpec((AT_DH, tm), lambda i: (0, i)), row64,
                   pl.BlockSpec((8, tm), lambda i: (0, i))],
        out_shape=[jax.ShapeDtypeStruct((AT_HEADS, AT_DH, R), BF16),
                   jax.ShapeDtypeStruct((IDX_HEADS, IDX_DIM, R), BF16),
                   jax.ShapeDtypeStruct((R, AT_DH), BF16), jax.ShapeDtypeStruct((AT_DH, R), BF16),
                   jax.ShapeDtypeStruct((R, IDX_DIM), BF16), jax.ShapeDtypeStruct((8, R), F32)],
        compiler_params=_params(("parallel",)),
        name="dsa_prep",
    )(zatt, cg, wuq_t, wqi_t, qg_col, kg, eye)


def _dsa_body(qt_ref, qit_ref, wt_ref, k_ref, vt_ref, ki_ref, wo_ref, tri_ref, o_ref,
              sc_ref, hi_ref, lo_ref, kaug_ref, qaug_ref, s_ref, p_ref, m_ref, acc_ref, *, n_sel):
    QB = Q_BLOCK
    G = DSA_GROUP
    I16 = jnp.int16
    qb = pl.program_id(1)
    ngrp = (qb + G) // G
    keyi = lax.broadcasted_iota(I32, (QB, QB), 0)
    qpos = qb * QB + lax.broadcasted_iota(I32, (QB, QB), 1)

    qits = [qit_ref[h] for h in range(IDX_HEADS)]
    wrows = [wt_ref[h:h + 1, :] for h in range(IDX_HEADS)]

    def score_grp(g, masked):
        for u in range(G):
            j = g * G + u
            kij = ki_ref[pl.ds(pl.multiple_of(j * QB, QB), QB), :]
            tot = jnp.zeros((QB, QB), F32)
            for h in range(IDX_HEADS):
                s = jnp.dot(kij, qits[h], preferred_element_type=F32)
                tot = tot + jnp.maximum(s, 0.0) * wrows[h]
            if masked:
                kpos = j * QB + keyi
                tot = jnp.where(kpos < PADF, MASK_BIG, tot)
                tot = jnp.where((kpos <= qpos) & (kpos >= DEAD), tot, -MASK_BIG)
            bits = pltpu.bitcast(tot + 0.0, I32)
            key = bits ^ ((bits >> 31) & 0x7FFFFFFF)
            sc_ref[j] = key
            hi_ref[j] = (key >> 16).astype(I16)
            lo_ref[j] = ((key & 0xFFFF) - 32768).astype(I16)

    score_grp(0, True)

    def interior(g, carry):
        score_grp(g, False)
        return carry

    lax.fori_loop(1, ngrp - 1, interior, 0)

    @pl.when(ngrp > 1)
    def _():
        score_grp(ngrp - 1, True)

    @pl.when(ngrp % 2 == 1)
    def _():
        for u in range(G):
            sc_ref[ngrp * G + u] = jnp.full((QB, QB), KEY_NEG, I32)
            hi_ref[ngrp * G + u] = jnp.full((QB, QB), -32768, I16)
            lo_ref[ngrp * G + u] = jnp.full((QB, QB), -32768, I16)

    one = jnp.ones((QB, QB), I16)
    zero = jnp.zeros((QB, QB), I16)

    def count16(ref, pred):
        def tree(xs):
            while len(xs) > 1:
                xs = [a + b for a, b in zip(xs[0::2], xs[1::2])]
            return xs[0]

        def cb(g, a):
            parts = []
            for u in range(2 * G):
                x = jnp.where(pred(ref[g * (2 * G) + u]), one, zero)
                parts.append(tree([x[16 * r:16 * r + 16] for r in range(QB // 16)]))
            return a + tree(parts)
        a = lax.fori_loop(0, (ngrp + 1) // 2, cb, jnp.zeros((16, QB), I16))
        return jnp.sum(a.astype(I32), axis=0, keepdims=True)

    def radix16(ref, want):
        def bit_body(bi, c):
            trial = c + lax.shift_left(jnp.int32(1), 15 - bi)
            t16 = trial.astype(I16)
            cnt = count16(ref, lambda x: x >= t16)
            return jnp.where(cnt >= want, trial, c)
        return lax.fori_loop(0, 16, bit_body, jnp.full((1, QB), -32768, I32))

    c_hi = radix16(hi_ref, n_sel)
    c_hi16 = c_hi.astype(I16)
    want_lo = n_sel - count16(hi_ref, lambda x: x > c_hi16)

    def band_grp(g, carry):
        for u in range(G):
            j = g * G + u
            lo_ref[j] = jnp.where(hi_ref[j] == c_hi16, lo_ref[j], jnp.full((QB, QB), -32768, I16))
        return carry

    lax.fori_loop(0, ngrp, band_grp, 0)
    c_lo = radix16(lo_ref, want_lo)
    c_lo16 = c_lo.astype(I16)
    c = (c_hi << 16) | (c_lo + 32768)
    need = (want_lo - count16(lo_ref, lambda x: x > c_lo16)).astype(F32)

    m_ref[...] = jnp.full(m_ref.shape, -3.0e38, F32)
    acc_ref[...] = jnp.zeros(acc_ref.shape, F32)
    tri = tri_ref[...]
    need = jnp.where(c > KEY_NEG, need, -1.0)

    rr = lax.broadcasted_iota(I32, (QB, 2 * QB), 0)
    cc = lax.broadcasted_iota(I32, (QB, 2 * QB), 1)
    eye2 = jnp.where((cc == rr) | (cc == rr + QB), 1.0, 0.0).astype(BF16)
    for hp in range(AT_HEADS // 2):
        qaug_ref[hp, 0:QB, :] = eye2
        qaug_ref[hp, QB:QB + AT_DH, 0:QB] = qt_ref[2 * hp]
        qaug_ref[hp, QB:QB + AT_DH, QB:2 * QB] = qt_ref[2 * hp + 1]

    def scores(g, eqc, slot):
        for u in range(G):
            key = sc_ref[g * G + u]
            eqf = jnp.where(key == c, 1.0, 0.0)
            pref = jnp.dot(tri, eqf.astype(BF16), preferred_element_type=F32)
            sel = (key > c) | ((key == c) & (pref <= need - eqc))
            eqc = eqc + jnp.sum(eqf, axis=0, keepdims=True)
            kaug_ref[slot, u * QB:(u + 1) * QB, 0:QB] = jnp.where(sel, 0.0, -MASK_BIG).astype(BF16)
        r0 = pl.multiple_of(g * (G * QB), G * QB)
        kaug_ref[slot, :, QB:QB + AT_DH] = k_ref[pl.ds(r0, G * QB), :]
        kaug = kaug_ref[slot]
        for hp in range(AT_HEADS // 2):
            s_ref[slot, hp] = jnp.dot(kaug, qaug_ref[hp], preferred_element_type=F32)
        return eqc

    def attend(g, slot):
        r0 = pl.multiple_of(g * (G * QB), G * QB)
        vtg = vt_ref[:, pl.ds(r0, G * QB)]
        alphas = []
        for hp in range(AT_HEADS // 2):
            hs = slice(hp * 2 * QB, (hp + 1) * 2 * QB)
            m_old = m_ref[:, hs]
            m_new = jnp.maximum(m_old, jnp.max(s_ref[slot, hp], axis=0, keepdims=True))
            m_ref[:, hs] = m_new
            p_ref[slot, hp] = jnp.exp2(s_ref[slot, hp] - m_new).astype(BF16)
            alphas.append(jnp.exp2(m_old - m_new))
        for hp in range(AT_HEADS // 2):
            hs = slice(hp * 2 * QB, (hp + 1) * 2 * QB)
            acc_ref[:, hs] = alphas[hp] * acc_ref[:, hs] + jnp.dot(vtg, p_ref[slot, hp],
                                                                   preferred_element_type=F32)

    npair = (ngrp + 1) // 2
    last = 2 * npair - 1

    def att_pair(i, eqc):
        eqc = scores(2 * i + 1, eqc, 1)
        attend(2 * i, 0)
        eqc_next = scores(jnp.minimum(2 * i + 2, last), eqc, 0)
        attend(2 * i + 1, 1)
        return eqc_next

    lax.fori_loop(0, npair, att_pair, scores(0, jnp.zeros((1, QB), F32), 0))

    ot = (acc_ref[0:AT_DH, :] / acc_ref[AT_DH:AT_DH + 1, :]).astype(BF16)
    y = jnp.zeros((QB, D_MODEL), F32)
    for h in range(AT_HEADS):
        y = y + lax.dot_general(ot[:, h * QB:(h + 1) * QB], wo_ref[h], _TN, preferred_element_type=F32)
    o_ref[...] = y.astype(BF16)


def _dsa_main(qt, qit, wt, k, vt, ki, wo, B, Tp, n_sel):
    R = k.shape[0]
    QB = Q_BLOCK
    nb = Tp // QB
    nbc = -(-nb // (2 * DSA_GROUP)) * (2 * DSA_GROUP)
    assert n_sel <= DSA_GROUP * QB
    Tx = nbc * QB
    pad_rows = lambda a: jnp.pad(a.reshape(B, Tp, -1), ((0, 0), (0, Tx - Tp), (0, 0))).reshape(B * Tx, -1)
    k, ki = pad_rows(k), pad_rows(ki)
    vt = jnp.pad(vt.reshape(AT_DH, B, Tp), ((0, 0), (0, 0), (0, Tx - Tp))).reshape(AT_DH, B * Tx)
    vt = jnp.concatenate([vt, jnp.ones((1, B * Tx), BF16), jnp.zeros((VT_ROWS - AT_DH - 1, B * Tx), BF16)], axis=0)
    tri = jnp.asarray(np.tril(np.ones((QB, QB), np.float32)), BF16)
    seq = lambda: pl.BlockSpec((Tx, 64), lambda b, i: (b, 0))
    return pl.pallas_call(
        functools.partial(_dsa_body, n_sel=n_sel),
        grid=(B, nb),
        in_specs=[pl.BlockSpec((AT_HEADS, AT_DH, QB), lambda b, i: (0, 0, b * nb + i)),
                  pl.BlockSpec((IDX_HEADS, IDX_DIM, QB), lambda b, i: (0, 0, b * nb + i)),
                  pl.BlockSpec((8, QB), lambda b, i: (0, b * nb + i)),
                  seq(), pl.BlockSpec((VT_ROWS, Tx), lambda b, i: (0, b)), seq(),
                  _const_spec(wo.shape), _const_spec(tri.shape)],
        out_specs=pl.BlockSpec((QB, D_MODEL), lambda b, i: (b * nb + i, 0)),
        out_shape=jax.ShapeDtypeStruct((R, D_MODEL), BF16),
        scratch_shapes=[pltpu.VMEM((nbc, QB, QB), I32), pltpu.VMEM((nbc, QB, QB), jnp.int16),
                        pltpu.VMEM((nbc, QB, QB), jnp.int16),
                        pltpu.VMEM((2, DSA_GROUP * QB, QB + AT_DH), BF16),
                        pltpu.VMEM((AT_HEADS // 2, QB + AT_DH, 2 * QB), BF16),
                        pltpu.VMEM((2, AT_HEADS // 2, DSA_GROUP * QB, 2 * QB), F32),
                        pltpu.VMEM((2, AT_HEADS // 2, DSA_GROUP * QB, 2 * QB), BF16),
                        pltpu.VMEM((1, AT_HEADS * QB), F32), pltpu.VMEM((VT_ROWS, AT_HEADS * QB), F32)],
        compiler_params=_params(("parallel", "arbitrary")),
        name="dsa_attention",
    )(qt, qit, wt, k, vt, ki, wo, tri)


def _mix_body(ohg_ref, yat_ref, ocv_ref, gt_ref, h_ref, whg_ref, wcv_ref, wmix_ref, o_ref):
    y_hg = jnp.dot(ohg_ref[...], whg_ref[...], preferred_element_type=F32)
    y_cv = jnp.dot(ocv_ref[...], wcv_ref[...], preferred_element_type=F32)
    y_at = yat_ref[...].astype(F32)
    g1 = _sigmoid(gt_ref[:, 0:D_MODEL].astype(F32))
    g2 = _sigmoid(gt_ref[:, D_MODEL:2 * D_MODEL].astype(F32))
    g3 = _sigmoid(gt_ref[:, 2 * D_MODEL:3 * D_MODEL].astype(F32))
    mixed = (g1 * y_hg + g2 * y_at + g3 * y_cv).astype(BF16)
    o_ref[...] = h_ref[...] + jnp.dot(mixed, wmix_ref[...], preferred_element_type=F32)


def _mix(ohg, yat, ocv, gates, h, whg, wcv, wmix):
    R, D = h.shape
    tm = _pick(R, (512, 256, 128))
    rows = lambda n: pl.BlockSpec((tm, n), lambda i: (i, 0))
    return pl.pallas_call(
        _mix_body,
        grid=(R // tm,),
        in_specs=[rows(ohg.shape[1]), rows(D), rows(ocv.shape[1]), rows(3 * D), rows(D),
                  _const_spec(whg.shape), _const_spec(wcv.shape), _const_spec(wmix.shape)],
        out_specs=rows(D),
        out_shape=jax.ShapeDtypeStruct((R, D), F32),
        compiler_params=_params(("parallel",)),
        name="branch_mix",
    )(ohg, yat, ocv, gates, h, whg, wcv, wmix)


def _ffn_body(h_ref, hh_ref, g_ref, wu_ref, dw_ref, db_ref, wd_ref, o_ref,
              xn_scr, ua_scr, ub_scr, acc_scr, *, tm):
    it = pl.program_id(1)
    g = g_ref[...]
    pos = it * tm - FF_HALO + lax.broadcasted_iota(I32, (tm + FF_HALO, 1), 0)
    live = pos >= DEAD
    xn_scr[0:FF_HALO] = jnp.where(live[0:FF_HALO], _rms(hh_ref[...], g), 0.0).astype(BF16)
    xn_scr[FF_HALO:FF_HALO + tm] = jnp.where(live[FF_HALO:], _rms(h_ref[...], g), 0.0).astype(BF16)
    acc_scr[...] = jnp.zeros(acc_scr.shape, F32)

    def conv(scr, slot, col0):
        cols = slice(col0, col0 + FF_CHUNK)
        y = db_ref[:, cols]
        for j in range(FF_CONV):
            r = FF_HALO - (FF_CONV - 1) + j
            y = y + dw_ref[j:j + 1, cols] * scr[slot, r:r + tm]
        return y

    xn = xn_scr[...]
    for c in range(N_FF_CHUNK):
        slot = c % 2
        ca, cb = c * FF_CHUNK, FF_DIM + c * FF_CHUNK
        ua_scr[slot] = jnp.dot(xn, wu_ref[:, ca:ca + FF_CHUNK], preferred_element_type=F32)
        ub_scr[slot] = jnp.dot(xn, wu_ref[:, cb:cb + FF_CHUNK], preferred_element_type=F32)
        a = conv(ua_scr, slot, ca)
        b = conv(ub_scr, slot, cb)
        act = (a * _sigmoid(a) * b).astype(BF16)
        acc_scr[...] += jnp.dot(act, wd_ref[ca:ca + FF_CHUNK, :], preferred_element_type=F32)
    o_ref[...] = h_ref[...] + acc_scr[...]


def _ffn(h, g, wu, dw, db, wd, B, Tp):
    R, D = h.shape
    tm = _pick(Tp, (640, 384, 128))
    nt = Tp // tm
    hb = tm // FF_HALO
    return pl.pallas_call(
        functools.partial(_ffn_body, tm=tm),
        grid=(B, nt),
        in_specs=[pl.BlockSpec((tm, D), lambda b, t: (b * nt + t, 0)),
                  pl.BlockSpec((FF_HALO, D), lambda b, t: (jnp.maximum((b * nt + t) * hb - 1, 0), 0)),
                  _const_spec((1, D))] + [_const_spec(w.shape) for w in (wu, dw, db, wd)],
        out_specs=pl.BlockSpec((tm, D), lambda b, t: (b * nt + t, 0)),
        out_shape=jax.ShapeDtypeStruct((R, D), F32),
        scratch_shapes=[pltpu.VMEM((FF_HALO + tm, D), BF16),
                        pltpu.VMEM((2, FF_HALO + tm, FF_CHUNK), F32), pltpu.VMEM((2, FF_HALO + tm, FF_CHUNK), F32),
                        pltpu.VMEM((tm, D), F32)],
        compiler_params=_params(("parallel", "parallel")),
        name="conv_ffn",
    )(h, h, g, wu, dw, db, wd)


def _pack_body(w_ref, hg_ref, at_ref, cv_ref, gt_ref):
    w = w_ref[0]
    HW = HG_HEADS * HG_DK
    hg_ref[...] = w[:, 0:4 * HW].astype(BF16)
    o = 4 * HW
    at_ref[:, 0:AT_QRANK] = w[:, o:o + AT_QRANK].astype(BF16)
    o += AT_QRANK
    for i, n in enumerate((AT_DH, AT_DH, IDX_DIM, IDX_HEADS)):
        piece = jnp.concatenate([w[:, o:o + n], jnp.zeros((w.shape[0], 128 - n), F32)], axis=1)
        at_ref[:, AT_QRANK + 128 * i:AT_QRANK + 128 * (i + 1)] = piece.astype(BF16)
        o += n
    cv_ref[...] = w[:, o:o + 2 * CV_CH].astype(BF16)
    o += 2 * CV_CH
    gt_ref[...] = w[:, o:o + N_BRANCH * D_MODEL].astype(BF16)


def _pack_in_proj(w_in, l):
    _, K, N = w_in.shape
    tk = 128
    widths = (4 * HG_HEADS * HG_DK, ATT_SLAB, 2 * CV_CH, N_BRANCH * D_MODEL)
    return pl.pallas_call(
        _pack_body,
        grid=(K // tk,),
        in_specs=[pl.BlockSpec((1, tk, N), lambda i: (l, i, 0))],
        out_specs=[pl.BlockSpec((tk, n), lambda i: (i, 0)) for n in widths],
        out_shape=[jax.ShapeDtypeStruct((K, n), BF16) for n in widths],
        compiler_params=_params(("parallel",)),
        name="pack_in_proj",
    )(w_in)


def kernel(x, meta_tokens, hgrn_lb, norm1_g, w_in, hg_norm_g, w_hg_out, cq_norm_g, w_uq, w_qi, q_norm_g, k_norm_g, w_at_out, cv_dw_w, cv_dw_b, cv_ln_g, cv_ln_b, w_cv_out, w_mix_out, norm2_g, w_ffn_up, ffn_dw_w, ffn_dw_b, w_ffn_down):
    B, SEQ, D = x.shape
    depth = w_in.shape[0]
    Tp = PADF + SEQ
    R = B * Tp
    n_sel = min(TOPK_MAX, SEQ // 4)

    meta = jnp.broadcast_to(meta_tokens.astype(x.dtype)[None], (B, N_META, D))
    h = jnp.concatenate([jnp.zeros((B, DEAD, D), x.dtype), meta, x], axis=1).reshape(R, D)
    p = jax.nn.softmax(hgrn_lb.astype(F32), axis=0)
    lbs = jnp.cumsum(p, axis=0) - p[0]

    row = lambda a: a.reshape(1, -1).astype(F32)
    for l in range(depth):
        w_hg, w_at, w_cv, w_gt = _pack_in_proj(w_in, l)
        zh, zatt, zu, zg = _in_proj(h, row(norm1_g[l]), w_hg, w_at, w_cv, w_gt)

        o_hg = _hgrn(zh, row(lbs[l]), row(hg_norm_g[l]), B, Tp)
        dw8 = jnp.broadcast_to(cv_dw_w[l].astype(F32)[:, None, :], (CV_WIDTH, 8, CV_CH))
        o_cv = _conv_branch(zu, dw8, row(cv_dw_b[l]), row(cv_ln_g[l]), row(cv_ln_b[l]), B, Tp)

        wuq_t = w_uq[l].reshape(AT_QRANK, AT_HEADS, AT_DH).transpose(1, 2, 0).astype(BF16)
        wqi_t = w_qi[l].reshape(AT_QRANK, IDX_HEADS, IDX_DIM).transpose(1, 2, 0).astype(BF16)
        qt, qit, k, vt, ki, wt = _dsa_prep(zatt, row(cq_norm_g[l]), wuq_t, wqi_t,
                                           q_norm_g[l].reshape(-1, 1).astype(F32), row(k_norm_g[l]))
        wo = w_at_out[l].reshape(AT_HEADS, AT_DH, D).astype(BF16)
        y_at = _dsa_main(qt, qit, wt, k, vt, ki, wo, B, Tp, n_sel)

        h = _mix(o_hg, y_at, o_cv, zg, h, w_hg_out[l].astype(BF16), w_cv_out[l].astype(BF16),
                 w_mix_out[l].astype(BF16))

        h = _ffn(h, row(norm2_g[l]), w_ffn_up[l].astype(BF16), ffn_dw_w[l].astype(F32), row(ffn_dw_b[l]),
                 w_ffn_down[l].astype(BF16), B, Tp)

    return h.reshape(B, Tp, D)[:, PADF:]
```

```python
import functools

import numpy as np
import jax
import jax.numpy as jnp
from jax import lax
from jax.experimental import pallas as pl
from jax.experimental.pallas import tpu as pltpu

F32 = jnp.float32
BF16 = jnp.bfloat16
I32 = jnp.int32

D_MODEL = 1024
N_META = 16
HG_HEADS = 4
HG_DK = 128
HG_DV = 128
HG_CHUNK = 64
AT_HEADS = 8
AT_DH = 64
AT_QRANK = 256
IDX_HEADS = 4
IDX_DIM = 64
TOPK_MAX = 256
Q_BLOCK = 128
CV_CH = 512
CV_WIDTH = 31
FF_DIM = 2816
FF_CONV = 3
N_BRANCH = 3
EPS = 1e-6
MASK_BIG = 1e30

PADF = Q_BLOCK
DEAD = PADF - N_META
FF_CHUNK = 256
N_FF_CHUNK = FF_DIM // FF_CHUNK
ATT_SLAB = 768
CV_HALO = 32
FF_HALO = 16
DSA_GROUP = 4
LOG2E = 1.4426950408889634
VT_ROWS = AT_DH + 16
KEY_NEG = int(np.array(-MASK_BIG, np.float32).view(np.int32)) ^ 0x7FFFFFFF
VMEM_LIMIT = 56 * 1024 * 1024

_NT = (((1,), (1,)), ((), ()))
_TN = (((0,), (0,)), ((), ()))


def _pick(n, cands):
    for c in cands:
        if n % c == 0:
            return c
    raise ValueError(f"no tile for {n}")


def _const_spec(shape):
    nd = len(shape)
    return pl.BlockSpec(shape, lambda *_: (0,) * nd, pipeline_mode=pl.Buffered(1))


def _params(sem):
    return pltpu.CompilerParams(dimension_semantics=sem, vmem_limit_bytes=VMEM_LIMIT)


def _sigmoid(x):
    return 1.0 / (1.0 + jnp.exp(-x))


def _rms(x, g):
    return x * lax.rsqrt(jnp.mean(x * x, axis=-1, keepdims=True) + EPS) * g


def _in_proj_body(h_ref, g_ref, w_hg, w_at, w_cv, w_gt, o_hg, o_at, o_cv, o_gt):
    xn = _rms(h_ref[...], g_ref[...]).astype(BF16)
    for w, o in ((w_hg, o_hg), (w_at, o_at), (w_cv, o_cv), (w_gt, o_gt)):
        o[...] = jnp.dot(xn, w[...], preferred_element_type=F32).astype(o.dtype)


def _in_proj(h, g, w_hg, w_at, w_cv, w_gt):
    R, D = h.shape
    tm = _pick(R, (256, 128))
    outs = ((w_hg.shape[1], F32), (w_at.shape[1], F32), (w_cv.shape[1], BF16), (w_gt.shape[1], BF16))
    return pl.pallas_call(
        _in_proj_body,
        grid=(R // tm,),
        in_specs=[pl.BlockSpec((tm, D), lambda i: (i, 0)), _const_spec((1, D))]
        + [_const_spec(w.shape) for w in (w_hg, w_at, w_cv, w_gt)],
        out_specs=[pl.BlockSpec((tm, n), lambda i: (i, 0)) for n, _ in outs],
        out_shape=[jax.ShapeDtypeStruct((R, n), dt) for n, dt in outs],
        compiler_params=_params(("parallel",)),
        name="in_proj",
    )(h, g, w_hg, w_at, w_cv, w_gt)


def _hgrn_consts():
    C = HG_CHUNK
    t = np.arange(C)[:, None]
    s = np.arange(C)[None, :]
    mats = [(s <= t)]
    for m in (8, 16, 32):
        rho = (t // (2 * m)) * (2 * m) + m - 1
        mats.append(s <= rho)
    return np.concatenate(mats, axis=0).astype(np.float32)


def _split3(x):
    hi = x.astype(BF16)
    r1 = x - hi.astype(F32)
    mid = r1.astype(BF16)
    lo = (r1 - mid.astype(F32)).astype(BF16)
    return hi, mid, lo


def _hgrn_body(q_ref, f_ref, i_ref, g_ref, lb_ref, ng_ref, cm_ref, o_ref, st_ref, *, tc, nbi):
    C = HG_CHUNK
    it = pl.program_id(1)

    @pl.when(it == 0)
    def _():
        st_ref[...] = jnp.zeros_like(st_ref)

    row = lax.broadcasted_iota(I32, (C, C), 0)
    col = lax.broadcasted_iota(I32, (C, C), 1)
    m_diag = (row // 8 == col // 8) & (col <= row)
    level_masks = []
    for m in (8, 16, 32):
        level_masks.append((row // (2 * m) == col // (2 * m)) & ((row // m) % 2 == 1) & ((col // m) % 2 == 0))
    rid = lax.broadcasted_iota(I32, (C, 1), 0)
    second = [(rid // m) % 2 == 1 for m in (8, 16, 32)]
    lane_c = lax.broadcasted_iota(I32, (8, C), 1)
    cm = cm_ref[...]
    ng = ng_ref[...]

    def chunk(c, carry):
        r0 = pl.multiple_of(c * C, C)
        live = (it * tc + r0 + rid) >= DEAD
        for bb, h in [(x, y) for x in range(nbi) for y in range(HG_HEADS)]:
            cs = slice(h * HG_DK, (h + 1) * HG_DK)
            lb = lb_ref[:, cs]
            f = lb + (1.0 - lb) * _sigmoid(f_ref[bb, pl.ds(r0, C), cs])
            lf = jnp.where(live, jnp.log2(f), 0.0)
            k = jnp.where(live, 1.0 - f, 0.0)
            qr = q_ref[bb, pl.ds(r0, C), cs]
            q = qr * _sigmoid(qr) * (HG_DK ** -0.5)
            v = i_ref[bb, pl.ds(r0, C), cs].astype(BF16)
            cums = sum(jnp.dot(cm, part, preferred_element_type=F32) for part in _split3(lf))
            b = cums[0:C]
            att = jnp.zeros((C, C), F32)
            for lvl in range(3):
                r = cums[(lvl + 1) * C:(lvl + 2) * C]
                e = jnp.exp2(jnp.minimum(jnp.where(second[lvl], b - r, r - b), 0.0))
                a = lax.dot_general((q * e).astype(BF16), (k * e).astype(BF16), _NT, preferred_element_type=F32)
                att = jnp.where(level_masks[lvl], a, att)
            blocks = []
            for j in range(C // 8):
                bR = b[8 * j:8 * j + 8]
                qR = q[8 * j:8 * j + 8]
                acc = jnp.zeros((8, C), F32)
                for s in range(8 * j, 8 * j + 8):
                    e = jnp.exp2(jnp.minimum(bR - b[s:s + 1], 0.0))
                    colv = jnp.sum(qR * (k[s:s + 1] * e), axis=-1, keepdims=True)
                    acc = jnp.where(lane_c == s, colv, acc)
                blocks.append(acc)
            att = jnp.where(m_diag, jnp.concatenate(blocks, axis=0), att)
            st = st_ref[bb * HG_HEADS + h]
            o = jnp.dot(att.astype(BF16), v, preferred_element_type=F32)
            o = o + lax.dot_general((q * jnp.exp2(b)).astype(BF16), st.astype(BF16), _NT, preferred_element_type=F32)
            bl = b[C - 1:C]
            kb = (k * jnp.exp2(bl - b)).astype(BF16)
            st_ref[bb * HG_HEADS + h] = st * jnp.exp2(bl) + lax.dot_general(v, kb, _TN, preferred_element_type=F32)
            gr = g_ref[bb, pl.ds(r0, C), cs]
            o_ref[bb, pl.ds(r0, C), cs] = (_rms(o, ng) * (gr * _sigmoid(gr))).astype(BF16)
        return carry

    lax.fori_loop(0, tc // C, chunk, 0)


def _hgrn(zh, lb, ng, B, Tp):
    R = zh.shape[0]
    W = HG_HEADS * HG_DK
    tc = _pick(Tp, (640, 384, 128))
    nt = Tp // tc
    cm = jnp.asarray(_hgrn_consts(), BF16)
    nbi = 2 if B % 2 == 0 else 1
    zh3 = zh.reshape(B, Tp, 4 * W)

    def col(cb):
        return pl.BlockSpec((nbi, tc, W), lambda b, t: (b, t, cb))

    out = pl.pallas_call(
        functools.partial(_hgrn_body, tc=tc, nbi=nbi),
        grid=(B // nbi, nt),
        in_specs=[col(0), col(1), col(2), col(3), _const_spec((1, W)), _const_spec((1, HG_DV)),
                  _const_spec(cm.shape)],
        out_specs=pl.BlockSpec((nbi, tc, W), lambda b, t: (b, t, 0)),
        out_shape=jax.ShapeDtypeStruct((B, Tp, W), BF16),
        scratch_shapes=[pltpu.VMEM((nbi * HG_HEADS, HG_DV, HG_DK), F32)],
        compiler_params=_params(("parallel", "arbitrary")),
        name="hgrn2",
    )(zh3, zh3, zh3, zh3, lb, ng, cm)
    return out.reshape(R, W)


def _conv_body(u_ref, uh_ref, dw_ref, db_ref, lg_ref, lb_ref, o_ref, scr, sh_scr, *, tt):
    it = pl.program_id(1)
    RC = 32
    L = tt + CV_HALO - 8

    def glu(u, pos0):
        u = u.astype(F32)
        hh = u[:, :CV_CH] * _sigmoid(u[:, CV_CH:])
        pos = pos0 + lax.broadcasted_iota(I32, (u.shape[0], 1), 0)
        return jnp.where(pos >= DEAD, hh, 0.0)

    scr[0:CV_HALO] = glu(uh_ref[...], it * tt - CV_HALO)
    scr[CV_HALO:CV_HALO + tt] = glu(u_ref[...], it * tt)
    for s in range(1, 8):
        sh_scr[s - 1] = scr[s:s + L]
    bias = db_ref[...]
    lg = lg_ref[...]
    lbias = lb_ref[...]

    def chunk(c, carry):
        r0 = pl.multiple_of(c * RC, RC)
        acc = jnp.zeros((RC // 8, 8, CV_CH), F32)
        for j in range(CV_WIDTH):
            off = CV_HALO - CV_WIDTH + 1 + j
            base = pl.ds(r0 + (off // 8) * 8, RC)
            xs = scr[base, :] if off % 8 == 0 else sh_scr[off % 8 - 1, base, :]
            acc = acc + dw_ref[j][None] * xs.reshape(RC // 8, 8, CV_CH)
        acc = acc.reshape(RC, CV_CH) + bias
        mu = jnp.mean(acc, axis=-1, keepdims=True)
        xc = acc - mu
        y = xc * lax.rsqrt(jnp.mean(xc * xc, axis=-1, keepdims=True) + EPS) * lg + lbias
        o_ref[pl.ds(r0, RC), :] = (y * _sigmoid(y)).astype(BF16)
        return carry

    lax.fori_loop(0, tt // RC, chunk, 0)


def _conv_branch(zu, dw, db, lg, lb, B, Tp):
    R = zu.shape[0]
    tt = _pick(Tp, (640, 384, 128))
    nt = Tp // tt
    hb = tt // CV_HALO
    return pl.pallas_call(
        functools.partial(_conv_body, tt=tt),
        grid=(B, nt),
        in_specs=[pl.BlockSpec((tt, 2 * CV_CH), lambda b, t: (b * nt + t, 0)),
                  pl.BlockSpec((CV_HALO, 2 * CV_CH), lambda b, t: (jnp.maximum((b * nt + t) * hb - 1, 0), 0)),
                  _const_spec(dw.shape), _const_spec((1, CV_CH)), _const_spec((1, CV_CH)), _const_spec((1, CV_CH))],
        out_specs=pl.BlockSpec((tt, CV_CH), lambda b, t: (b * nt + t, 0)),
        out_shape=jax.ShapeDtypeStruct((R, CV_CH), BF16),
        scratch_shapes=[pltpu.VMEM((CV_HALO + tt, CV_CH), F32), pltpu.VMEM((7, CV_HALO + tt - 8, CV_CH), F32)],
        compiler_params=_params(("parallel", "parallel")),
        name="conformer_conv",
    )(zu, zu, dw, db, lg, lb)


def _dsa_prep_body(z_ref, cg_ref, wuq_ref, wqi_ref, qg_ref, kg_ref, eye_ref,
                   q_ref, qi_ref, k_ref, vt_ref, ki_ref, w_ref):
    z = z_ref[...]
    cqn = _rms(z[:, 0:AT_QRANK], cg_ref[...]).astype(BF16)
    qg = qg_ref[...]
    for h in range(AT_HEADS):
        qh = lax.dot_general(wuq_ref[h], cqn, _NT, preferred_element_type=F32)
        qh = qh * lax.rsqrt(jnp.mean(qh * qh, axis=0, keepdims=True) + EPS) * qg
        q_ref[h] = (qh * (AT_DH ** -0.5 * LOG2E)).astype(BF16)
    for h in range(IDX_HEADS):
        qi_ref[h] = lax.dot_general(wqi_ref[h], cqn, _NT, preferred_element_type=F32).astype(BF16)
    k_ref[...] = _rms(z[:, 256:256 + AT_DH], kg_ref[...]).astype(BF16)
    vt_ref[...] = lax.dot_general(eye_ref[...], z[:, 384:384 + AT_DH].astype(BF16), _NT,
                                  preferred_element_type=F32).astype(BF16)
    ki_ref[...] = z[:, 512:512 + IDX_DIM].astype(BF16)
    w_ref[...] = z[:, 640:768].T[0:8, :] * ((IDX_HEADS * IDX_DIM) ** -0.5)


def _dsa_prep(zatt, cg, wuq_t, wqi_t, qg_col, kg):
    R = zatt.shape[0]
    tm = _pick(R, (512, 256, 128))
    row64 = pl.BlockSpec((tm, 64), lambda i: (i, 0))
    eye = jnp.eye(AT_DH, dtype=BF16)
    return pl.pallas_call(
        _dsa_prep_body,
        grid=(R // tm,),
        in_specs=[pl.BlockSpec((tm, ATT_SLAB), lambda i: (i, 0)), _const_spec((1, AT_QRANK)),
                  _const_spec(wuq_t.shape), _const_spec(wqi_t.shape), _const_spec((AT_DH, 1)),
                  _const_spec((1, AT_DH)), _const_spec(eye.shape)],
        out_specs=[pl.BlockSpec((AT_HEADS, AT_DH, tm), lambda i: (0, 0, i)),
                   pl.BlockSpec((IDX_HEADS, IDX_DIM, tm), lambda i: (0, 0, i)),
                   row64, pl.BlockSpec((AT_DH, tm), lambda i: (0, i)), row64,
                   pl.BlockSpec((8, tm), lambda i: (0, i))],
        out_shape=[jax.ShapeDtypeStruct((AT_HEADS, AT_DH, R), BF16),
                   jax.ShapeDtypeStruct((IDX_HEADS, IDX_DIM, R), BF16),
                   jax.ShapeDtypeStruct((R, AT_DH), BF16), jax.ShapeDtypeStruct((AT_DH, R), BF16),
                   jax.ShapeDtypeStruct((R, IDX_DIM), BF16), jax.ShapeDtypeStruct((8, R), F32)],
        compiler_params=_params(("parallel",)),
        name="dsa_prep",
    )(zatt, cg, wuq_t, wqi_t, qg_col, kg, eye)


def _dsa_body(qt_ref, qit_ref, wt_ref, k_ref, vt_ref, ki_ref, wo_ref, tri_ref, o_ref,
              sc_ref, hi_ref, lo_ref, kaug_ref, qaug_ref, s_ref, p_ref, m_ref, acc_ref, *, n_sel):
    QB = Q_BLOCK
    G = DSA_GROUP
    I16 = jnp.int16
    qb = pl.program_id(1)
    ngrp = (qb + G) // G
    keyi = lax.broadcasted_iota(I32, (QB, QB), 0)
    qpos = qb * QB + lax.broadcasted_iota(I32, (QB, QB), 1)

    qits = [qit_ref[h] for h in range(IDX_HEADS)]
    wrows = [wt_ref[h:h + 1, :] for h in range(IDX_HEADS)]

    def score_grp(g, masked):
        for u in range(G):
            j = g * G + u
            kij = ki_ref[pl.ds(pl.multiple_of(j * QB, QB), QB), :]
            tot = jnp.zeros((QB, QB), F32)
            for h in range(IDX_HEADS):
                s = jnp.dot(kij, qits[h], preferred_element_type=F32)
                tot = tot + jnp.maximum(s, 0.0) * wrows[h]
            if masked:
                kpos = j * QB + keyi
                tot = jnp.where(kpos < PADF, MASK_BIG, tot)
                tot = jnp.where((kpos <= qpos) & (kpos >= DEAD), tot, -MASK_BIG)
            bits = pltpu.bitcast(tot + 0.0, I32)
            key = bits ^ ((bits >> 31) & 0x7FFFFFFF)
            sc_ref[j] = key
            hi_ref[j] = (key >> 16).astype(I16)
            lo_ref[j] = ((key & 0xFFFF) - 32768).astype(I16)

    score_grp(0, True)

    def interior(g, carry):
        score_grp(g, False)
        return carry

    lax.fori_loop(1, ngrp - 1, interior, 0)

    @pl.when(ngrp > 1)
    def _():
        score_grp(ngrp - 1, True)

    @pl.when(ngrp % 2 == 1)
    def _():
        for u in range(G):
            sc_ref[ngrp * G + u] = jnp.full((QB, QB), KEY_NEG, I32)
            hi_ref[ngrp * G + u] = jnp.full((QB, QB), -32768, I16)
            lo_ref[ngrp * G + u] = jnp.full((QB, QB), -32768, I16)

    one = jnp.ones((QB, QB), I16)
    zero = jnp.zeros((QB, QB), I16)

    def count16(ref, pred):
        def tree(xs):
            while len(xs) > 1:
                xs = [a + b for a, b in zip(xs[0::2], xs[1::2])]
            return xs[0]

        def cb(g, a):
            parts = []
            for u in range(2 * G):
                x = jnp.where(pred(ref[g * (2 * G) + u]), one, zero)
                parts.append(tree([x[16 * r:16 * r + 16] for r in range(QB // 16)]))
            return a + tree(parts)
        a = lax.fori_loop(0, (ngrp + 1) // 2, cb, jnp.zeros((16, QB), I16))
        return jnp.sum(a.astype(I32), axis=0, keepdims=True)

    def radix16(ref, want):
        def bit_body(bi, c):
            trial = c + lax.shift_left(jnp.int32(1), 15 - bi)
            t16 = trial.astype(I16)
            cnt = count16(ref, lambda x: x >= t16)
            return jnp.where(cnt >= want, trial, c)
        return lax.fori_loop(0, 16, bit_body, jnp.full((1, QB), -32768, I32))

    c_hi = radix16(hi_ref, n_sel)
    c_hi16 = c_hi.astype(I16)
    want_lo = n_sel - count16(hi_ref, lambda x: x > c_hi16)

    def band_grp(g, carry):
        for u in range(G):
            j = g * G + u
            lo_ref[j] = jnp.where(hi_ref[j] == c_hi16, lo_ref[j], jnp.full((QB, QB), -32768, I16))
        return carry

    lax.fori_loop(0, ngrp, band_grp, 0)
    c_lo = radix16(lo_ref, want_lo)
    c_lo16 = c_lo.astype(I16)
    c = (c_hi << 16) | (c_lo + 32768)
    need = (want_lo - count16(lo_ref, lambda x: x > c_lo16)).astype(F32)

    m_ref[...] = jnp.full(m_ref.shape, -3.0e38, F32)
    acc_ref[...] = jnp.zeros(acc_ref.shape, F32)
    tri = tri_ref[...]
    need = jnp.where(c > KEY_NEG, need, -1.0)

    rr = lax.broadcasted_iota(I32, (QB, 2 * QB), 0)
    cc = lax.broadcasted_iota(I32, (QB, 2 * QB), 1)
    eye2 = jnp.where((cc == rr) | (cc == rr + QB), 1.0, 0.0).astype(BF16)
    for hp in range(AT_HEADS // 2):
        qaug_ref[hp, 0:QB, :] = eye2
        qaug_ref[hp, QB:QB + AT_DH, 0:QB] = qt_ref[2 * hp]
        qaug_ref[hp, QB:QB + AT_DH, QB:2 * QB] = qt_ref[2 * hp + 1]

    def scores(g, eqc, slot):
        for u in range(G):
            key = sc_ref[g * G + u]
            eqf = jnp.where(key == c, 1.0, 0.0)
            pref = jnp.dot(tri, eqf.astype(BF16), preferred_element_type=F32)
            sel = (key > c) | ((key == c) & (pref <= need - eqc))
            eqc = eqc + jnp.sum(eqf, axis=0, keepdims=True)
            kaug_ref[slot, u * QB:(u + 1) * QB, 0:QB] = jnp.where(sel, 0.0, -MASK_BIG).astype(BF16)
        r0 = pl.multiple_of(g * (G * QB), G * QB)
        kaug_ref[slot, :, QB:QB + AT_DH] = k_ref[pl.ds(r0, G * QB), :]
        kaug = kaug_ref[slot]
        for hp in range(AT_HEADS // 2):
            s_ref[slot, hp] = jnp.dot(kaug, qaug_ref[hp], preferred_element_type=F32)
        return eqc

    def attend(g, slot):
        r0 = pl.multiple_of(g * (G * QB), G * QB)
        vtg = vt_ref[:, pl.ds(r0, G * QB)]
        alphas = []
        for hp in range(AT_HEADS // 2):
            hs = slice(hp * 2 * QB, (hp + 1) * 2 * QB)
            m_old = m_ref[:, hs]
            m_new = jnp.maximum(m_old, jnp.max(s_ref[slot, hp], axis=0, keepdims=True))
            m_ref[:, hs] = m_new
            p_ref[slot, hp] = jnp.exp2(s_ref[slot, hp] - m_new).astype(BF16)
            alphas.append(jnp.exp2(m_old - m_new))
        for hp in range(AT_HEADS // 2):
            hs = slice(hp * 2 * QB, (hp + 1) * 2 * QB)
            acc_ref[:, hs] = alphas[hp] * acc_ref[:, hs] + jnp.dot(vtg, p_ref[slot, hp],
                                                                   preferred_element_type=F32)

    npair = (ngrp + 1) // 2
    last = 2 * npair - 1

    def att_pair(i, eqc):
        eqc = scores(2 * i + 1, eqc, 1)
        attend(2 * i, 0)
        eqc_next = scores(jnp.minimum(2 * i + 2, last), eqc, 0)
        attend(2 * i + 1, 1)
        return eqc_next

    lax.fori_loop(0, npair, att_pair, scores(0, jnp.zeros((1, QB), F32), 0))

    ot = (acc_ref[0:AT_DH, :] / acc_ref[AT_DH:AT_DH + 1, :]).astype(BF16)
    y = jnp.zeros((QB, D_MODEL), F32)
    for h in range(AT_HEADS):
        y = y + lax.dot_general(ot[:, h * QB:(h + 1) * QB], wo_ref[h], _TN, preferred_element_type=F32)
    o_ref[...] = y.astype(BF16)


def _dsa_main(qt, qit, wt, k, vt, ki, wo, B, Tp, n_sel):
    R = k.shape[0]
    QB = Q_BLOCK
    nb = Tp // QB
    nbc = -(-nb // (2 * DSA_GROUP)) * (2 * DSA_GROUP)
    assert n_sel <= DSA_GROUP * QB
    Tx = nbc * QB
    pad_rows = lambda a: jnp.pad(a.reshape(B, Tp, -1), ((0, 0), (0, Tx - Tp), (0, 0))).reshape(B * Tx, -1)
    k, ki = pad_rows(k), pad_rows(ki)
    vt = jnp.pad(vt.reshape(AT_DH, B, Tp), ((0, 0), (0, 0), (0, Tx - Tp))).reshape(AT_DH, B * Tx)
    vt = jnp.concatenate([vt, jnp.ones((1, B * Tx), BF16), jnp.zeros((VT_ROWS - AT_DH - 1, B * Tx), BF16)], axis=0)
    tri = jnp.asarray(np.tril(np.ones((QB, QB), np.float32)), BF16)
    seq = lambda: pl.BlockSpec((Tx, 64), lambda b, i: (b, 0))
    return pl.pallas_call(
        functools.partial(_dsa_body, n_sel=n_sel),
        grid=(B, nb),
        in_specs=[pl.BlockSpec((AT_HEADS, AT_DH, QB), lambda b, i: (0, 0, b * nb + i)),
                  pl.BlockSpec((IDX_HEADS, IDX_DIM, QB), lambda b, i: (0, 0, b * nb + i)),
                  pl.BlockSpec((8, QB), lambda b, i: (0, b * nb + i)),
                  seq(), pl.BlockSpec((VT_ROWS, Tx), lambda b, i: (0, b)), seq(),
                  _const_spec(wo.shape), _const_spec(tri.shape)],
        out_specs=pl.BlockSpec((QB, D_MODEL), lambda b, i: (b * nb + i, 0)),
        out_shape=jax.ShapeDtypeStruct((R, D_MODEL), BF16),
        scratch_shapes=[pltpu.VMEM((nbc, QB, QB), I32), pltpu.VMEM((nbc, QB, QB), jnp.int16),
                        pltpu.VMEM((nbc, QB, QB), jnp.int16),
                        pltpu.VMEM((2, DSA_GROUP * QB, QB + AT_DH), BF16),
                        pltpu.VMEM((AT_HEADS // 2, QB + AT_DH, 2 * QB), BF16),
                        pltpu.VMEM((2, AT_HEADS // 2, DSA_GROUP * QB, 2 * QB), F32),
                        pltpu.VMEM((2, AT_HEADS // 2, DSA_GROUP * QB, 2 * QB), BF16),
                        pltpu.VMEM((1, AT_HEADS * QB), F32), pltpu.VMEM((VT_ROWS, AT_HEADS * QB), F32)],
        compiler_params=_params(("parallel", "arbitrary")),
        name="dsa_attention",
    )(qt, qit, wt, k, vt, ki, wo, tri)


def _mix_body(ohg_ref, yat_ref, ocv_ref, gt_ref, h_ref, whg_ref, wcv_ref, wmix_ref, o_ref):
    y_hg = jnp.dot(ohg_ref[...], whg_ref[...], preferred_element_type=F32)
    y_cv = jnp.dot(ocv_ref[...], wcv_ref[...], preferred_element_type=F32)
    y_at = yat_ref[...].astype(F32)
    g1 = _sigmoid(gt_ref[:, 0:D_MODEL].astype(F32))
    g2 = _sigmoid(gt_ref[:, D_MODEL:2 * D_MODEL].astype(F32))
    g3 = _sigmoid(gt_ref[:, 2 * D_MODEL:3 * D_MODEL].astype(F32))
    mixed = (g1 * y_hg + g2 * y_at + g3 * y_cv).astype(BF16)
    o_ref[...] = h_ref[...] + jnp.dot(mixed, wmix_ref[...], preferred_element_type=F32)


def _mix(ohg, yat, ocv, gates, h, whg, wcv, wmix):
    R, D = h.shape
    tm = _pick(R, (512, 256, 128))
    rows = lambda n: pl.BlockSpec((tm, n), lambda i: (i, 0))
    return pl.pallas_call(
        _mix_body,
        grid=(R // tm,),
        in_specs=[rows(ohg.shape[1]), rows(D), rows(ocv.shape[1]), rows(3 * D), rows(D),
                  _const_spec(whg.shape), _const_spec(wcv.shape), _const_spec(wmix.shape)],
        out_specs=rows(D),
        out_shape=jax.ShapeDtypeStruct((R, D), F32),
        compiler_params=_params(("parallel",)),
        name="branch_mix",
    )(ohg, yat, ocv, gates, h, whg, wcv, wmix)


def _ffn_body(h_ref, hh_ref, g_ref, wu_ref, dw_ref, db_ref, wd_ref, o_ref,
              xn_scr, ua_scr, ub_scr, acc_scr, *, tm):
    it = pl.program_id(1)
    g = g_ref[...]
    pos = it * tm - FF_HALO + lax.broadcasted_iota(I32, (tm + FF_HALO, 1), 0)
    live = pos >= DEAD
    xn_scr[0:FF_HALO] = jnp.where(live[0:FF_HALO], _rms(hh_ref[...], g), 0.0).astype(BF16)
    xn_scr[FF_HALO:FF_HALO + tm] = jnp.where(live[FF_HALO:], _rms(h_ref[...], g), 0.0).astype(BF16)
    acc_scr[...] = jnp.zeros(acc_scr.shape, F32)

    def conv(scr, slot, col0):
        cols = slice(col0, col0 + FF_CHUNK)
        y = db_ref[:, cols]
        for j in range(FF_CONV):
            r = FF_HALO - (FF_CONV - 1) + j
            y = y + dw_ref[j:j + 1, cols] * scr[slot, r:r + tm]
        return y

    xn = xn_scr[...]
    for c in range(N_FF_CHUNK):
        slot = c % 2
        ca, cb = c * FF_CHUNK, FF_DIM + c * FF_CHUNK
        ua_scr[slot] = jnp.dot(xn, wu_ref[:, ca:ca + FF_CHUNK], preferred_element_type=F32)
        ub_scr[slot] = jnp.dot(xn, wu_ref[:, cb:cb + FF_CHUNK], preferred_element_type=F32)
        a = conv(ua_scr, slot, ca)
        b = conv(ub_scr, slot, cb)
        act = (a * _sigmoid(a) * b).astype(BF16)
        acc_scr[...] += jnp.dot(act, wd_ref[ca:ca + FF_CHUNK, :], preferred_element_type=F32)
    o_ref[...] = h_ref[...] + acc_scr[...]


def _ffn(h, g, wu, dw, db, wd, B, Tp):
    R, D = h.shape
    tm = _pick(Tp, (640, 384, 128))
    nt = Tp // tm
    hb = tm // FF_HALO
    return pl.pallas_call(
        functools.partial(_ffn_body, tm=tm),
        grid=(B, nt),
        in_specs=[pl.BlockSpec((tm, D), lambda b, t: (b * nt + t, 0)),
                  pl.BlockSpec((FF_HALO, D), lambda b, t: (jnp.maximum((b * nt + t) * hb - 1, 0), 0)),
                  _const_spec((1, D))] + [_const_spec(w.shape) for w in (wu, dw, db, wd)],
        out_specs=pl.BlockSpec((tm, D), lambda b, t: (b * nt + t, 0)),
        out_shape=jax.ShapeDtypeStruct((R, D), F32),
        scratch_shapes=[pltpu.VMEM((FF_HALO + tm, D), BF16),
                        pltpu.VMEM((2, FF_HALO + tm, FF_CHUNK), F32), pltpu.VMEM((2, FF_HALO + tm, FF_CHUNK), F32),
                        pltpu.VMEM((tm, D), F32)],
        compiler_params=_params(("parallel", "parallel")),
        name="conv_ffn",
    )(h, h, g, wu, dw, db, wd)


def _pack_body(w_ref, hg_ref, at_ref, cv_ref, gt_ref):
    w = w_ref[0]
    HW = HG_HEADS * HG_DK
    hg_ref[...] = w[:, 0:4 * HW].astype(BF16)
    o = 4 * HW
    at_ref[:, 0:AT_QRANK] = w[:, o:o + AT_QRANK].astype(BF16)
    o += AT_QRANK
    for i, n in enumerate((AT_DH, AT_DH, IDX_DIM, IDX_HEADS)):
        piece = jnp.concatenate([w[:, o:o + n], jnp.zeros((w.shape[0], 128 - n), F32)], axis=1)
        at_ref[:, AT_QRANK + 128 * i:AT_QRANK + 128 * (i + 1)] = piece.astype(BF16)
        o += n
    cv_ref[...] = w[:, o:o + 2 * CV_CH].astype(BF16)
    o += 2 * CV_CH
    gt_ref[...] = w[:, o:o + N_BRANCH * D_MODEL].astype(BF16)


def _pack_in_proj(w_in, l):
    _, K, N = w_in.shape
    tk = 128
    widths = (4 * HG_HEADS * HG_DK, ATT_SLAB, 2 * CV_CH, N_BRANCH * D_MODEL)
    return pl.pallas_call(
        _pack_body,
        grid=(K // tk,),
        in_specs=[pl.BlockSpec((1, tk, N), lambda i: (l, i, 0))],
        out_specs=[pl.BlockSpec((tk, n), lambda i: (i, 0)) for n in widths],
        out_shape=[jax.ShapeDtypeStruct((K, n), BF16) for n in widths],
        compiler_params=_params(("parallel",)),
        name="pack_in_proj",
    )(w_in)


def kernel(x, meta_tokens, hgrn_lb, norm1_g, w_in, hg_norm_g, w_hg_out, cq_norm_g, w_uq, w_qi, q_norm_g, k_norm_g, w_at_out, cv_dw_w, cv_dw_b, cv_ln_g, cv_ln_b, w_cv_out, w_mix_out, norm2_g, w_ffn_up, ffn_dw_w, ffn_dw_b, w_ffn_down):
    B, SEQ, D = x.shape
    depth = w_in.shape[0]
    Tp = PADF + SEQ
    R = B * Tp
    n_sel = min(TOPK_MAX, SEQ // 4)

    meta = jnp.broadcast_to(meta_tokens.astype(x.dtype)[None], (B, N_META, D))
    h = jnp.concatenate([jnp.zeros((B, DEAD, D), x.dtype), meta, x], axis=1).reshape(R, D)
    p = jax.nn.softmax(hgrn_lb.astype(F32), axis=0)
    lbs = jnp.cumsum(p, axis=0) - p[0]

    row = lambda a: a.reshape(1, -1).astype(F32)
    for l in range(depth):
        w_hg, w_at, w_cv, w_gt = _pack_in_proj(w_in, l)
        zh, zatt, zu, zg = _in_proj(h, row(norm1_g[l]), w_hg, w_at, w_cv, w_gt)

        o_hg = _hgrn(zh, row(lbs[l]), row(hg_norm_g[l]), B, Tp)
        dw8 = jnp.broadcast_to(cv_dw_w[l].astype(F32)[:, None, :], (CV_WIDTH, 8, CV_CH))
        o_cv = _conv_branch(zu, dw8, row(cv_dw_b[l]), row(cv_ln_g[l]), row(cv_ln_b[l]), B, Tp)

        wuq_t = w_uq[l].reshape(AT_QRANK, AT_HEADS, AT_DH).transpose(1, 2, 0).astype(BF16)
        wqi_t = w_qi[l].reshape(AT_QRANK, IDX_HEADS, IDX_DIM).transpose(1, 2, 0).astype(BF16)
        qt, qit, k, vt, ki, wt = _dsa_prep(zatt, row(cq_norm_g[l]), wuq_t, wqi_t,
                                           q_norm_g[l].reshape(-1, 1).astype(F32), row(k_norm_g[l]))
        wo = w_at_out[l].reshape(AT_HEADS, AT_DH, D).astype(BF16)
        y_at = _dsa_main(qt, qit, wt, k, vt, ki, wo, B, Tp, n_sel)

        h = _mix(o_hg, y_at, o_cv, zg, h, w_hg_out[l].astype(BF16), w_cv_out[l].astype(BF16),
                 w_mix_out[l].astype(BF16))

        h = _ffn(h, row(norm2_g[l]), w_ffn_up[l].astype(BF16), ffn_dw_w[l].astype(F32), row(ffn_dw_b[l]),
                 w_ffn_down[l].astype(BF16), B, Tp)

    return h.reshape(B, Tp, D)[:, PADF:]
```
